```python
import jax
import jax.numpy as jnp
from jax import lax
import numpy as np

D_MODEL = 1024
BATCH = 2
SEQ = 16384
DEPTH = 4

GRID_W = 64
CTX_LEN = 256
HEAD_DIM = 64
ROPE_BASE = 10000.0
QBLK = 128
NORM_EPS = 1e-6
NEG_INF = -1e30
A_HEADS = 8
A_KV_HEADS = 2
A_WINDOW = 128
B_HEADS = 8
NA_ROWS = 8
NA_COLS = 16
C_HEADS = 8
C_Q_RANK = 768
C_KV_RANK = 256
C_NOPE = 64
C_ROPE = 32
C_V = 64
D_HEADS = 8
D_KV_HEADS = 2
N_EXPERTS = 32
TOP_K = 4
D_EXPERT = 1024
SWIGLU_LIMIT = 7.0
SWIGLU_ALPHA = 1.702
MOE_BLK = 256
N_EVEN = (DEPTH + 1) // 2
N_ODD = DEPTH // 2
AB_IN = 2304
AB_OUT = 1024
CD_IN = 1824
CD_OUT = 1024

kernel_name = 'hybrid_diffusion_trunk'


def _rms_norm(x, g):
    xf = x.astype(jnp.float32)
    y = xf * lax.rsqrt(jnp.mean(xf * xf, axis=-1, keepdims=True) + NORM_EPS)
    return (y * g.astype(jnp.float32)).astype(x.dtype)


def _modulate(x, g, shift, scale):
    return _rms_norm(x, g) * (1 + scale) + shift


def _axial_rope(n_tokens, dim, dtype):
    t = jnp.arange(n_tokens, dtype=jnp.int32)
    row = (t // GRID_W).astype(jnp.float32)
    col = (t % GRID_W).astype(jnp.float32)
    quarter = dim // 4
    inv_freq = ROPE_BASE ** (-jnp.arange(quarter, dtype=jnp.float32) / quarter)
    ang = jnp.concatenate([row[:, None] * inv_freq, col[:, None] * inv_freq], axis=-1)
    return (jnp.cos(ang)[:, None, :].astype(dtype), jnp.sin(ang)[:, None, :].astype(dtype))


def _apply_rope(x, rope):
    cos, sin = rope
    half = x.shape[-1] // 2
    x1, x2 = x[..., :half], x[..., half:]
    return jnp.concatenate([x1 * cos - x2 * sin, x1 * sin + x2 * cos], axis=-1)


def _group(q, n_groups):
    return q.reshape(q.shape[:2] + (n_groups, q.shape[2] // n_groups, q.shape[3]))


def _scores(q, k):
    return jnp.einsum('bqgrd,bkgd->bgrqk', q, k).astype(jnp.float32)


def _mix(p, v):
    return jnp.einsum('bgrqk,bkgd->bqgrd', p.astype(v.dtype), v)


def _sink_column(sink, like):
    return jnp.broadcast_to(sink.astype(jnp.float32)[None, :, :, None, None], like.shape[:-1] + (1,))


def _dense_attention(q, k, v, sink=None):
    s = _scores(q, k) * (q.shape[-1] ** -0.5)
    if sink is None:
        return _mix(jax.nn.softmax(s, axis=-1), v)
    p = jax.nn.softmax(jnp.concatenate([s, _sink_column(sink, s)], axis=-1), axis=-1)
    return _mix(p[..., :-1], v)


def _to_blocks(q):
    b, t = q.shape[:2]
    return jnp.swapaxes(q.reshape((b, t // QBLK, QBLK) + q.shape[2:]), 0, 1)


def _from_blocks(o):
    o = jnp.swapaxes(o, 0, 1)
    return o.reshape(o.shape[0], o.shape[1] * o.shape[2], -1)


def _blocked_dense_attention(q, k, v):
    return _from_blocks(lax.map(lambda qb: _dense_attention(qb, k, v), _to_blocks(q)))


def _window_sink_attention(q, k, v, kc, vc, sink):
    n = q.shape[1]
    span = QBLK + 2 * A_WINDOW
    pad = ((0, 0), (A_WINDOW, A_WINDOW), (0, 0), (0, 0))
    kp, vp = jnp.pad(k, pad), jnp.pad(v, pad)
    scale = q.shape[-1] ** -0.5
    n_ctx = kc.shape[1]

    def block(args):
        i, qb = args
        start = i * QBLK
        kb = lax.dynamic_slice_in_dim(kp, start, span, axis=1)
        vb = lax.dynamic_slice_in_dim(vp, start, span, axis=1)
        qpos = start + jnp.arange(QBLK)
        kpos = start - A_WINDOW + jnp.arange(span)
        ok = (jnp.abs(qpos[:, None] - kpos[None, :]) <= A_WINDOW) & ((kpos >= 0) & (kpos < n))[None, :]
        s_loc = jnp.where(ok, _scores(qb, kb) * scale, NEG_INF)
        s_ctx = _scores(qb, kc) * scale
        p = jax.nn.softmax(jnp.concatenate([s_loc, s_ctx, _sink_column(sink, s_ctx)], axis=-1), axis=-1)
        return _mix(p[..., :span], vb) + _mix(p[..., span:span + n_ctx], vc)

    out = lax.map(block, (jnp.arange(n // QBLK), _to_blocks(q)))
    return _from_blocks(out)


def _neighbourhood_attention(q, k, v, kc, vc, rpb, rows):
    b, n, h, d = q.shape
    kh = min(NA_ROWS, rows)
    kw = NA_COLS
    scale = d ** -0.5
    kg = k.reshape(b, rows, GRID_W, h, d)
    vg = v.reshape(b, rows, GRID_W, h, d)
    qr = jnp.swapaxes(q.reshape(b, rows, GRID_W, h, d), 0, 1)
    cols = jnp.arange(GRID_W)
    col_idx = jnp.clip(cols - kw // 2, 0, GRID_W - kw)[:, None] + jnp.arange(kw)[None, :]
    dc = col_idx - cols[:, None] + (NA_COLS - 1)
    rpb_f = rpb.astype(jnp.float32)

    def row_block(args):
        r, qb = args
        rs = jnp.clip(r - kh // 2, 0, rows - kh)
        kr = lax.dynamic_slice_in_dim(kg, rs, kh, axis=1)[:, :, col_idx]
        vr = lax.dynamic_slice_in_dim(vg, rs, kh, axis=1)[:, :, col_idx]
        dr = rs + jnp.arange(kh) - r + (NA_ROWS - 1)
        bias = rpb_f[:, dr[None, :, None], dc[:, None, :]]
        s_loc = jnp.einsum('bchd,bicjhd->bhcij', qb, kr).astype(jnp.float32) * scale + bias[None]
        s_loc = s_loc.reshape(b, h, GRID_W, kh * kw)
        s_ctx = jnp.einsum('bchd,bkhd->bhck', qb, kc).astype(jnp.float32) * scale
        p = jax.nn.softmax(jnp.concatenate([s_loc, s_ctx], axis=-1), axis=-1)
        p_loc = p[..., :kh * kw].reshape(b, h, GRID_W, kh, kw).astype(v.dtype)
        p_ctx = p[..., kh * kw:].astype(v.dtype)
        return (jnp.einsum('bhcij,bicjhd->bchd', p_loc, vr)
                + jnp.einsum('bhck,bkhd->bchd', p_ctx, vc))

    out = lax.map(row_block, (jnp.arange(rows), qr))
    return jnp.swapaxes(out, 0, 1).reshape(b, n, h * d)


def _mixer_ab(hc, hl, w_in, w_out, sink, rpb, rope, rows, need_ctx):
    d = HEAD_DIM
    splits = [A_HEADS * d, (A_HEADS + A_KV_HEADS) * d, (A_HEADS + 2 * A_KV_HEADS) * d,
              (A_HEADS + 2 * A_KV_HEADS + B_HEADS) * d, (A_HEADS + 2 * A_KV_HEADS + 2 * B_HEADS) * d]

    def project(h):
        b, t = h.shape[:2]
        qa, ka, va, qb, kb, vb = jnp.split(h @ w_in, splits, axis=-1)
        return (qa.reshape(b, t, A_HEADS, d), ka.reshape(b, t, A_KV_HEADS, d), va.reshape(b, t, A_KV_HEADS, d),
                qb.reshape(b, t, B_HEADS, d), kb.reshape(b, t, B_HEADS, d), vb.reshape(b, t, B_HEADS, d))

    qa_c, ka_c, va_c, qb_c, kb_c, vb_c = project(hc)
    qa, ka, va, qb, kb, vb = project(hl)
    sink_gr = sink.reshape(A_KV_HEADS, A_HEADS // A_KV_HEADS)
    ya = _window_sink_attention(_group(_apply_rope(qa, rope), A_KV_HEADS), _apply_rope(ka, rope), va,
                                ka_c, va_c, sink_gr)
    yb = _neighbourhood_attention(qb, kb, vb, kb_c, vb_c, rpb, rows)
    yl = jnp.concatenate([ya, yb], axis=-1) @ w_out
    if not need_ctx:
        return None, yl
    b, t = hc.shape[:2]
    ya_c = _dense_attention(_group(qa_c, A_KV_HEADS), ka_c, va_c, sink_gr).reshape(b, t, -1)
    yb_c = _dense_attention(qb_c[:, :, :, None], kb_c, vb_c).reshape(b, t, -1)
    return jnp.concatenate([ya_c, yb_c], axis=-1) @ w_out, yl


def _mixer_cd(hc, hl, w_in, q_norm, w_q_b, kv_norm, w_kv_b, dq_norm, dk_norm, w_out, rope_mla, rope_head,
              need_ctx):
    d = HEAD_DIM
    base = C_Q_RANK + C_KV_RANK + C_ROPE
    splits = [C_Q_RANK, C_Q_RANK + C_KV_RANK, base, base + D_HEADS * d, base + (D_HEADS + D_KV_HEADS) * d]

    def project(h, positioned):
        b, t = h.shape[:2]
        cq, ckv, k_rope, qd, kd, vd = jnp.split(h @ w_in, splits, axis=-1)
        q = (_rms_norm(cq, q_norm) @ w_q_b).reshape(b, t, C_HEADS, C_NOPE + C_ROPE)
        kv = (_rms_norm(ckv, kv_norm) @ w_kv_b).reshape(b, t, C_HEADS, C_NOPE + C_V)
        q_nope, q_rope = q[..., :C_NOPE], q[..., C_NOPE:]
        k_nope, v_mla = kv[..., :C_NOPE], kv[..., C_NOPE:]
        k_rope = k_rope[:, :, None, :]
        qd = _rms_norm(qd.reshape(b, t, D_HEADS, d), dq_norm)
        kd = _rms_norm(kd.reshape(b, t, D_KV_HEADS, d), dk_norm)
        vd = vd.reshape(b, t, D_KV_HEADS, d)
        if positioned:
            q_rope, k_rope = _apply_rope(q_rope, rope_mla), _apply_rope(k_rope, rope_mla)
            qd, kd = _apply_rope(qd, rope_head), _apply_rope(kd, rope_head)
        q_mla = jnp.concatenate([q_nope, q_rope], axis=-1)[:, :, :, None]
        k_mla = jnp.concatenate([k_nope, jnp.broadcast_to(k_rope, (b, t, C_HEADS, C_ROPE))], axis=-1)
        return q_mla, k_mla, v_mla, _group(qd, D_KV_HEADS), kd, vd

    qc_c, kc_c, vc_c, qd_c, kd_c, vd_c = project(hc, False)
    qc, kc, vc, qd, kd, vd = project(hl, True)
    yc = _blocked_dense_attention(qc, jnp.concatenate([kc_c, kc], axis=1), jnp.concatenate([vc_c, vc], axis=1))
    yd = _blocked_dense_attention(qd, jnp.concatenate([kd_c, kd], axis=1), jnp.concatenate([vd_c, vd], axis=1))
    yl = jnp.concatenate([yc, yd], axis=-1) @ w_out
    if not need_ctx:
        return None, yl
    b, t = hc.shape[:2]
    yc_c = _dense_attention(qc_c, kc_c, vc_c).reshape(b, t, -1)
    yd_c = _dense_attention(qd_c, kd_c, vd_c).reshape(b, t, -1)
    return jnp.concatenate([yc_c, yd_c], axis=-1) @ w_out, yl


def _clamped_swiglu(u):
    glu, lin = u[..., :D_EXPERT], u[..., D_EXPERT:]
    glu = jnp.minimum(glu, SWIGLU_LIMIT)
    lin = jnp.clip(lin, -SWIGLU_LIMIT, SWIGLU_LIMIT)
    return glu * jax.nn.sigmoid(SWIGLU_ALPHA * glu) * (lin + 1)


def _moe(h, router_w, router_b, w_in, b_in, w_out, b_out):
    n, dm = h.shape
    nk = n * TOP_K
    logits = (h @ router_w).astype(jnp.float32) + router_b.astype(jnp.float32)
    top_val, top_idx = lax.top_k(logits, TOP_K)
    gate = jax.nn.softmax(top_val, axis=-1).astype(h.dtype)
    expert = top_idx.reshape(-1)
    token = jnp.arange(nk, dtype=jnp.int32) // TOP_K
    order = jnp.argsort(expert)
    e_sorted = expert[order]
    sizes = jnp.bincount(expert, length=N_EXPERTS)
    padded = (sizes + MOE_BLK - 1) // MOE_BLK * MOE_BLK
    starts = jnp.cumsum(sizes) - sizes
    pends = jnp.cumsum(padded)
    pstarts = pends - padded
    dest = pstarts[e_sorted] + jnp.arange(nk, dtype=jnp.int32) - starts[e_sorted]
    n_blocks = -(-(nk + N_EXPERTS * (MOE_BLK - 1)) // MOE_BLK)
    tok_buf = jnp.full((n_blocks * MOE_BLK,), n, jnp.int32).at[dest].set(token[order])
    gate_buf = jnp.zeros((n_blocks * MOE_BLK,), h.dtype).at[dest].set(gate.reshape(-1)[order])
    blk_expert = jnp.minimum(jnp.searchsorted(pends, jnp.arange(n_blocks, dtype=jnp.int32) * MOE_BLK,
                                              side='right'), N_EXPERTS - 1)
    h_pad = jnp.concatenate([h, jnp.zeros((1, dm), h.dtype)], axis=0)

    def step(acc, xs):
        idx, g, e = xs
        u = h_pad[idx] @ w_in[e] + b_in[e]
        out = _clamped_swiglu(u) @ w_out[e] + b_out[e]
        return acc.at[idx].add(out * g[:, None]), None

    acc, _ = lax.scan(step, jnp.zeros((n + 1, dm), h.dtype),
                      (tok_buf.reshape(n_blocks, MOE_BLK), gate_buf.reshape(n_blocks, MOE_BLK), blk_expert))
    return acc[:n]


def setup_inputs(seed: int = 0) -> dict:
    key = jax.random.key(seed)
    ks = iter(jax.random.split(key, 32))
    dm = D_MODEL

    def normal(shape, scale):
        return jax.random.normal(next(ks), shape, jnp.float32) * scale

    return {
        'x': normal((BATCH, SEQ, dm), 1.0),
        'c': normal((BATCH, dm), 1.0),
        'ctx': normal((BATCH, CTX_LEN, dm), 1.0),
        'c_ctx': normal((dm,), 1.0),
        'mod_w': normal((DEPTH, dm, 6 * dm), 0.5 * dm ** -0.5),
        'mod_b': normal((DEPTH, 6 * dm), 0.02),
        'norm_mix': 1.0 + normal((DEPTH, dm), 0.05),
        'norm_ffn': 1.0 + normal((DEPTH, dm), 0.05),
        'ab_w_in': normal((N_EVEN, dm, AB_IN), dm ** -0.5),
        'ab_w_out': normal((N_EVEN, AB_OUT, dm), AB_OUT ** -0.5),
        'a_sink': normal((N_EVEN, A_HEADS), 0.5),
        'b_rpb': normal((N_EVEN, B_HEADS, 2 * NA_ROWS - 1, 2 * NA_COLS - 1), 0.1),
        'cd_w_in': normal((N_ODD, dm, CD_IN), dm ** -0.5),
        'c_q_norm': 1.0 + normal((N_ODD, C_Q_RANK), 0.05),
        'c_w_q_b': normal((N_ODD, C_Q_RANK, C_HEADS * (C_NOPE + C_ROPE)), C_Q_RANK ** -0.5),
        'c_kv_norm': 1.0 + normal((N_ODD, C_KV_RANK), 0.05),
        'c_w_kv_b': normal((N_ODD, C_KV_RANK, C_HEADS * (C_NOPE + C_V)), C_KV_RANK ** -0.5),
        'd_q_norm': 1.0 + normal((N_ODD, HEAD_DIM), 0.05),
        'd_k_norm': 1.0 + normal((N_ODD, HEAD_DIM), 0.05),
        'cd_w_out': normal((N_ODD, CD_OUT, dm), CD_OUT ** -0.5),
        'router_w': normal((DEPTH, dm, N_EXPERTS), dm ** -0.5),
        'router_b': normal((DEPTH, N_EXPERTS), 0.01),
        'exp_w_in': normal((DEPTH, N_EXPERTS, dm, 2 * D_EXPERT), dm ** -0.5),
        'exp_b_in': normal((DEPTH, N_EXPERTS, 2 * D_EXPERT), 0.02),
        'exp_w_out': normal((DEPTH, N_EXPERTS, D_EXPERT, dm), D_EXPERT ** -0.5),
        'exp_b_out': normal((DEPTH, N_EXPERTS, dm), 0.02),
        'final_norm': 1.0 + normal((dm,), 0.05),
    }


def reference(x, c, ctx, c_ctx, mod_w, mod_b, norm_mix, norm_ffn, ab_w_in, ab_w_out, a_sink, b_rpb,
              cd_w_in, c_q_norm, c_w_q_b, c_kv_norm, c_w_kv_b, d_q_norm, d_k_norm, cd_w_out,
              router_w, router_b, exp_w_in, exp_b_in, exp_w_out, exp_b_out, final_norm):
    b, t, dm = x.shape
    n_ctx = ctx.shape[1]
    rows = t // GRID_W
    rope_head = _axial_rope(t, HEAD_DIM, x.dtype)
    rope_mla = _axial_rope(t, C_ROPE, x.dtype)
    cond_lat = jax.nn.silu(c)[:, None, :]
    cond_ctx = jax.nn.silu(c_ctx)
    xc = ctx
    for layer in range(DEPTH):
        need_ctx = layer < DEPTH - 1
        i = layer // 2
        mod_lat = jnp.split(cond_lat @ mod_w[layer] + mod_b[layer], 6, axis=-1)
        mod_ctx = jnp.split(cond_ctx @ mod_w[layer] + mod_b[layer], 6, axis=-1)
        hl = _modulate(x, norm_mix[layer], mod_lat[0], mod_lat[1])
        hc = _modulate(xc, norm_mix[layer], mod_ctx[0], mod_ctx[1])
        if layer % 2 == 0:
            yc, yl = _mixer_ab(hc, hl, ab_w_in[i], ab_w_out[i], a_sink[i], b_rpb[i], rope_head, rows, need_ctx)
        else:
            yc, yl = _mixer_cd(hc, hl, cd_w_in[i], c_q_norm[i], c_w_q_b[i], c_kv_norm[i], c_w_kv_b[i],
                               d_q_norm[i], d_k_norm[i], cd_w_out[i], rope_mla, rope_head, need_ctx)
        x = x + mod_lat[2] * yl
        hl = _modulate(x, norm_ffn[layer], mod_lat[3], mod_lat[4])
        moe_w = (router_w[layer], router_b[layer], exp_w_in[layer], exp_b_in[layer], exp_w_out[layer],
                 exp_b_out[layer])
        if need_ctx:
            xc = xc + mod_ctx[2] * yc
            hc = _modulate(xc, norm_ffn[layer], mod_ctx[3], mod_ctx[4])
            y = _moe(jnp.concatenate([hc.reshape(-1, dm), hl.reshape(-1, dm)], axis=0), *moe_w)
            xc = xc + mod_ctx[5] * y[:b * n_ctx].reshape(b, n_ctx, dm)
            x = x + mod_lat[5] * y[b * n_ctx:].reshape(b, t, dm)
        else:
            x = x + mod_lat[5] * _moe(hl.reshape(-1, dm), *moe_w).reshape(b, t, dm)
    return _rms_norm(x, final_norm)
```

```python
import functools

import jax
import jax.numpy as jnp
import numpy as np
from jax import lax
from jax.experimental import pallas as pl
from jax.experimental.pallas import tpu as pltpu

GRID_W = 64
HEAD_DIM = 64
ROPE_BASE = 10000.0
NORM_EPS = 1e-6
NEG_INF = -1e30
A_HEADS = 8
A_KV_HEADS = 2
A_WINDOW = 128
B_HEADS = 8
NA_ROWS = 8
NA_COLS = 16
C_HEADS = 8
C_Q_RANK = 768
C_KV_RANK = 256
C_NOPE = 64
C_ROPE = 32
C_V = 64
D_HEADS = 8
D_KV_HEADS = 2
N_EXPERTS = 32
TOP_K = 4
D_EXPERT = 1024
SWIGLU_LIMIT = 7.0
SWIGLU_ALPHA = 1.702

LANES = 128
VMEM_LIMIT_BYTES = 56 * 1024 * 1024

MXU_DTYPE = jnp.bfloat16

ROW_TILE = 640
MOE_ROWS = 256
COMBINE_ROWS = 256
NA_QROWS = 8
NA_KROWS = 16
WIN_Q = 256
FLASH_M = 512
FLASH_TK = 1280


def _cparams(sem, vmem=VMEM_LIMIT_BYTES):
    return pltpu.CompilerParams(dimension_semantics=sem, vmem_limit_bytes=vmem)


def _dot(a, b):
    return jnp.dot(a, b, preferred_element_type=jnp.float32)


def _dot_nt(a, b):
    return lax.dot_general(a, b, (((1,), (1,)), ((), ())), preferred_element_type=jnp.float32)


def _mod_kernel(c_ref, w_ref, b_ref, o_ref):
    c = c_ref[...]
    s = c * jax.nn.sigmoid(c)
    o_ref[...] = jnp.dot(s, w_ref[...], preferred_element_type=jnp.float32,
                         precision=lax.Precision.HIGHEST) + b_ref[...]


def _modulation(cond, mod_w, mod_b):
    depth, dm, n = mod_w.shape
    tn = 1536
    return pl.pallas_call(
        _mod_kernel,
        grid=(depth, n // tn),
        in_specs=[
            pl.BlockSpec((8, dm), lambda l, j: (0, 0)),
            pl.BlockSpec((None, dm, tn), lambda l, j: (l, 0, j)),
            pl.BlockSpec((None, 1, tn), lambda l, j: (l, 0, j)),
        ],
        out_specs=pl.BlockSpec((None, 8, tn), lambda l, j: (l, 0, j)),
        out_shape=jax.ShapeDtypeStruct((depth, 8, n), jnp.float32),
        compiler_params=_cparams(("arbitrary", "arbitrary")),
        name="adaln_modulation",
    )(cond, mod_w, mod_b.reshape(depth, 1, n))


def _modulated_norm(x, mod_ref, row0, n_lat, a_lat, b_lat, a_ctx, b_ctx):
    tm = x.shape[0]
    ms = jnp.mean(x * x, axis=-1, keepdims=True)
    xh = x * lax.rsqrt(ms + NORM_EPS)
    rows = row0 + lax.broadcasted_iota(jnp.int32, (tm, 1), 0)
    is_ctx = rows >= n_lat
    a = jnp.where(is_ctx, mod_ref[a_ctx:a_ctx + 1, :], mod_ref[a_lat:a_lat + 1, :])
    b = jnp.where(is_ctx, mod_ref[b_ctx:b_ctx + 1, :], mod_ref[b_lat:b_lat + 1, :])
    return xh * a + b


def _store_heads(o_ref, val, n_heads, width, lane0=0):
    for h in range(n_heads):
        o_ref[h, :, lane0:lane0 + width] = val[:, h * width:(h + 1) * width].astype(o_ref.dtype)


def _tile_lanes(t, reps):
    return t if reps == 1 else jnp.concatenate([t] * reps, axis=-1)


def _proj_ab_kernel(x_ref, mod_ref, w_ref, cos_ref, sin_ref,
                    qa_ref, ka_ref, va_ref, qb_ref, kb_ref, vb_ref, *, tm, n_lat):
    row0 = pl.program_id(1) * tm
    h = _modulated_norm(x_ref[...], mod_ref, row0, n_lat, 0, 1, 2, 3).astype(MXU_DTYPE)
    cos = cos_ref[...]
    sin = sin_ref[...]
    d = HEAD_DIM
    scale = d ** -0.5
    qa = _dot(h, w_ref[:, 0:512]) * _tile_lanes(cos, 4) + _dot(h, w_ref[:, 2304:2816]) * _tile_lanes(sin, 4)
    _store_heads(qa_ref, qa * scale, A_HEADS, d)
    ka = _dot(h, w_ref[:, 512:640]) * cos + _dot(h, w_ref[:, 2816:2944]) * sin
    _store_heads(ka_ref, ka, A_KV_HEADS, d)
    _store_heads(va_ref, _dot(h, w_ref[:, 640:768]), A_KV_HEADS, d)
    _store_heads(qb_ref, _dot(h, w_ref[:, 768:1280]) * scale, B_HEADS, d)
    _store_heads(kb_ref, _dot(h, w_ref[:, 1280:1792]), B_HEADS, d)
    _store_heads(vb_ref, _dot(h, w_ref[:, 1792:2304]), B_HEADS, d)


def _proj_ab(xs, modv, w_ext, cos_t, sin_t, n_lat):
    b, s, dm = xs.shape
    tm = ROW_TILE
    d = HEAD_DIM
    nw = w_ext.shape[1]

    def head_spec(nh):
        return pl.BlockSpec((None, nh, tm, d), lambda bi, j: (bi, 0, j, 0))

    def head_shape(nh):
        return jax.ShapeDtypeStruct((b, nh, s, d), MXU_DTYPE)

    return pl.pallas_call(
        functools.partial(_proj_ab_kernel, tm=tm, n_lat=n_lat),
        grid=(b, s // tm),
        in_specs=[
            pl.BlockSpec((None, tm, dm), lambda bi, j: (bi, j, 0)),
            pl.BlockSpec((None, 8, dm), lambda bi, j: (bi, 0, 0)),
            pl.BlockSpec((dm, nw), lambda bi, j: (0, 0)),
            pl.BlockSpec((tm, LANES), lambda bi, j: (j, 0)),
            pl.BlockSpec((tm, LANES), lambda bi, j: (j, 0)),
        ],
        out_specs=[head_spec(A_HEADS), head_spec(A_KV_HEADS), head_spec(A_KV_HEADS),
                   head_spec(B_HEADS), head_spec(B_HEADS), head_spec(B_HEADS)],
        out_shape=[head_shape(A_HEADS), head_shape(A_KV_HEADS), head_shape(A_KV_HEADS),
                   head_shape(B_HEADS), head_shape(B_HEADS), head_shape(B_HEADS)],
        compiler_params=_cparams(("arbitrary", "arbitrary")),
        name="proj_ab",
    )(xs, modv, w_ext, cos_t, sin_t)


def _proj_cd_kernel(x_ref, mod_ref, w1_ref, wq2_ref, wkv2_ref, qn_ref, kvn_ref, gq_ref, gk_ref,
                    cos_ref, sin_ref, cosm_ref, sinm_ref,
                    qc_ref, kc_ref, vc_ref, qd_ref, kd_ref, vd_ref, *, tm, n_lat):
    row0 = pl.program_id(1) * tm
    h = _modulated_norm(x_ref[...], mod_ref, row0, n_lat, 0, 1, 2, 3).astype(MXU_DTYPE)
    cos = cos_ref[...]
    sin = sin_ref[...]
    cosm = cosm_ref[...]
    sinm = sinm_ref[...]
    d = HEAD_DIM

    cq = _dot(h, w1_ref[:, 0:768])
    cqn = cq * lax.rsqrt(jnp.mean(cq * cq, axis=-1, keepdims=True) + NORM_EPS) * qn_ref[...]
    q2 = _dot(cqn.astype(MXU_DTYPE), wq2_ref[...])
    c_scale = (C_NOPE + C_ROPE) ** -0.5
    q_rope = (q2[:, 512:768] * _tile_lanes(cosm, 2) + q2[:, 768:1024] * _tile_lanes(sinm, 2)) * c_scale
    _store_heads(qc_ref, q2[:, 0:512] * c_scale, C_HEADS, C_NOPE)
    _store_heads(qc_ref, q_rope, C_HEADS, C_ROPE, lane0=C_NOPE)

    ckv = _dot(h, w1_ref[:, 768:1024])
    ckvn = ckv * lax.rsqrt(jnp.mean(ckv * ckv, axis=-1, keepdims=True) + NORM_EPS) * kvn_ref[...]
    kv2 = _dot(ckvn.astype(MXU_DTYPE), wkv2_ref[...])
    k_rope = (_dot(h, w1_ref[:, 2432:2560]) * cosm + _dot(h, w1_ref[:, 2560:2688]) * sinm)[:, 0:C_ROPE]
    _store_heads(kc_ref, kv2[:, 0:512], C_HEADS, C_NOPE)
    for hh in range(C_HEADS):
        kc_ref[hh, :, C_NOPE:C_NOPE + C_ROPE] = k_rope.astype(kc_ref.dtype)
    _store_heads(vc_ref, kv2[:, 512:1024], C_HEADS, C_V)

    def normed_rope(raw, swapped_gained, gain, n_heads, o_ref, scale):
        reps = n_heads * d // LANES
        val = raw * (_tile_lanes(cos, reps) * gain) + swapped_gained * _tile_lanes(sin, reps)
        for hh in range(n_heads):
            sl = raw[:, hh * d:(hh + 1) * d]
            r = lax.rsqrt(jnp.mean(sl * sl, axis=-1, keepdims=True) + NORM_EPS)
            o_ref[hh, :, :] = (val[:, hh * d:(hh + 1) * d] * (r * scale)).astype(o_ref.dtype)

    normed_rope(_dot(h, w1_ref[:, 1024:1536]), _dot(h, w1_ref[:, 1792:2304]), gq_ref[...], D_HEADS, qd_ref, d ** -0.5)
    normed_rope(_dot(h, w1_ref[:, 1536:1664]), _dot(h, w1_ref[:, 2304:2432]), gk_ref[...], D_KV_HEADS, kd_ref, 1.0)
    _store_heads(vd_ref, _dot(h, w1_ref[:, 1664:1792]), D_KV_HEADS, d)


def _proj_cd(xs, modv, w1, wq2, wkv2, qn, kvn, gq, gk, cos_t, sin_t, cosm_t, sinm_t, n_lat):
    b, s, dm = xs.shape
    tm = ROW_TILE
    dk_c = C_NOPE + C_ROPE

    def full(a):
        return pl.BlockSpec(a.shape, lambda bi, j: (0,) * a.ndim)

    def head_spec(nh, w):
        return pl.BlockSpec((None, nh, tm, w), lambda bi, j: (bi, 0, j, 0))

    def head_shape(nh, w):
        return jax.ShapeDtypeStruct((b, nh, s, w), MXU_DTYPE)

    tab = pl.BlockSpec((tm, LANES), lambda bi, j: (j, 0))
    return pl.pallas_call(
        functools.partial(_proj_cd_kernel, tm=tm, n_lat=n_lat),
        grid=(b, s // tm),
        in_specs=[
            pl.BlockSpec((None, tm, dm), lambda bi, j: (bi, j, 0)),
            pl.BlockSpec((None, 8, dm), lambda bi, j: (bi, 0, 0)),
            full(w1), full(wq2), full(wkv2), full(qn), full(kvn), full(gq), full(gk),
            tab, tab, tab, tab,
        ],
        out_specs=[head_spec(C_HEADS, dk_c), head_spec(C_HEADS, dk_c), head_spec(C_HEADS, C_V),
                   head_spec(D_HEADS, HEAD_DIM), head_spec(D_KV_HEADS, HEAD_DIM), head_spec(D_KV_HEADS, HEAD_DIM)],
        out_shape=[head_shape(C_HEADS, dk_c), head_shape(C_HEADS, dk_c), head_shape(C_HEADS, C_V),
                   head_shape(D_HEADS, HEAD_DIM), head_shape(D_KV_HEADS, HEAD_DIM), head_shape(D_KV_HEADS, HEAD_DIM)],
        compiler_params=_cparams(("arbitrary", "arbitrary")),
        name="proj_cd",
    )(xs, modv, w1, wq2, wkv2, qn, kvn, gq, gk, cos_t, sin_t, cosm_t, sinm_t)


def _flash_kernel(*refs, r, tq, tk, nk, has_sink):
    if has_sink:
        q_ref, k_ref, v_ref, sink_ref, o_ref, m_sc, l_sc, acc_sc = refs
    else:
        q_ref, k_ref, v_ref, o_ref, m_sc, l_sc, acc_sc = refs
    m_rows = r * tq
    q = q_ref[...].reshape(m_rows, q_ref.shape[-1])
    if has_sink:
        m_sc[...] = sink_ref[...]
        l_sc[...] = jnp.ones_like(l_sc)
    else:
        m_sc[...] = jnp.full_like(m_sc, NEG_INF)
        l_sc[...] = jnp.zeros_like(l_sc)
    acc_sc[...] = jnp.zeros_like(acc_sc)

    def body(j, carry):
        k0 = pl.multiple_of(j * tk, tk)
        k = k_ref[pl.ds(k0, tk), :]
        v = v_ref[pl.ds(k0, tk), :]
        s = _dot_nt(q, k)
        m_prev = m_sc[...]
        m_new = jnp.maximum(m_prev, jnp.max(s, axis=-1, keepdims=True))
        alpha = jnp.exp(m_prev - m_new)
        p = jnp.exp(s - m_new)
        l_sc[...] = alpha * l_sc[...] + jnp.sum(p, axis=-1, keepdims=True)
        acc_sc[...] = alpha * acc_sc[...] + _dot(p.astype(MXU_DTYPE), v)
        m_sc[...] = m_new
        return carry

    lax.fori_loop(0, nk, body, 0)
    o = acc_sc[...] / l_sc[...]
    o_ref[...] = o.reshape(o_ref.shape).astype(o_ref.dtype)


def _flash(q, k, v, *, groups, tq, tk, q_row0, n_q, kv_row0, n_kv, sink=None, out=None):
    b, hq, s, dk = q.shape
    dv = v.shape[-1]
    r = hq // groups
    q5 = q.reshape(b, groups, r, s, dk)
    nk = n_kv // tk
    has_sink = sink is not None
    m_rows = r * tq
    in_specs = [
        pl.BlockSpec((None, None, r, tq, dk), lambda bi, g, i: (bi, g, 0, q_row0 // tq + i, 0)),
        pl.BlockSpec((None, None, n_kv, dk), lambda bi, g, i: (bi, g, kv_row0 // n_kv, 0)),
        pl.BlockSpec((None, None, n_kv, dv), lambda bi, g, i: (bi, g, kv_row0 // n_kv, 0)),
    ]
    args = [q5, k, v]
    if has_sink:
        in_specs.append(pl.BlockSpec((None, m_rows, 1), lambda bi, g, i: (g, 0, 0)))
        args.append(sink)
    aliases = {}
    if out is not None:
        in_specs.append(pl.BlockSpec(memory_space=pl.ANY))
        args.append(out.reshape(b, groups, r, s, dv))
        aliases = {len(args) - 1: 0}
    kern = functools.partial(_flash_kernel, r=r, tq=tq, tk=tk, nk=nk, has_sink=has_sink)
    if out is not None:
        inner = kern
        kern = lambda *refs: inner(*refs[:len(args) - 1], *refs[len(args):])
    o = pl.pallas_call(
        kern,
        grid=(b, groups, n_q // tq),
        in_specs=in_specs,
        out_specs=pl.BlockSpec((None, None, r, tq, dv), lambda bi, g, i: (bi, g, 0, q_row0 // tq + i, 0)),
        out_shape=jax.ShapeDtypeStruct((b, groups, r, s, dv), MXU_DTYPE),
        scratch_shapes=[pltpu.VMEM((m_rows, 1), jnp.float32), pltpu.VMEM((m_rows, 1), jnp.float32),
                        pltpu.VMEM((m_rows, dv), jnp.float32)],
        input_output_aliases=aliases,
        compiler_params=_cparams(("arbitrary", "arbitrary", "arbitrary")),
        name="dense_attention",
    )(*args)
    return o.reshape(b, hq, s, dv)


def _window_kernel(q_ref, k_ref, v_ref, sink_ref, init_ref, o_ref, *, r, tq, n_lat, n_ctx):
    del init_ref
    i = pl.program_id(2)
    span = tq + 2 * A_WINDOW
    d = q_ref.shape[-1]
    q = q_ref[...].reshape(r * tq, d)
    start = pl.multiple_of(jnp.clip(i * tq - A_WINDOW, 0, n_lat - span), A_WINDOW)
    k_loc = k_ref[pl.ds(start, span), :]
    v_loc = v_ref[pl.ds(start, span), :]
    k_ctx = k_ref[n_lat:n_lat + n_ctx, :]
    v_ctx = v_ref[n_lat:n_lat + n_ctx, :]
    qpos = i * tq + lax.rem(lax.broadcasted_iota(jnp.int32, (r * tq, span), 0), tq)
    kpos = start + lax.broadcasted_iota(jnp.int32, (r * tq, span), 1)
    ok = jnp.abs(qpos - kpos) <= A_WINDOW
    s_loc = jnp.where(ok, _dot_nt(q, k_loc), NEG_INF)
    s_ctx = _dot_nt(q, k_ctx)
    sink = sink_ref[...]
    m = jnp.maximum(jnp.maximum(jnp.max(s_loc, axis=-1, keepdims=True), jnp.max(s_ctx, axis=-1, keepdims=True)), sink)
    p_loc = jnp.exp(s_loc - m)
    p_ctx = jnp.exp(s_ctx - m)
    denom = jnp.sum(p_loc, axis=-1, keepdims=True) + jnp.sum(p_ctx, axis=-1, keepdims=True) + jnp.exp(sink - m)
    o = (_dot(p_loc.astype(MXU_DTYPE), v_loc) + _dot(p_ctx.astype(MXU_DTYPE), v_ctx)) / denom
    o_ref[...] = o.reshape(o_ref.shape).astype(o_ref.dtype)


def _window_attention(q, k, v, sink_col, n_lat, n_ctx):
    b, hq, s, d = q.shape
    g = k.shape[1]
    r = hq // g
    tq = WIN_Q
    q5 = q.reshape(b, g, r, s, d)
    o = pl.pallas_call(
        functools.partial(_window_kernel, r=r, tq=tq, n_lat=n_lat, n_ctx=n_ctx),
        grid=(b, g, n_lat // tq),
        in_specs=[
            pl.BlockSpec((None, None, r, tq, d), lambda bi, gi, i: (bi, gi, 0, i, 0)),
            pl.BlockSpec((None, None, s, d), lambda bi, gi, i: (bi, gi, 0, 0)),
            pl.BlockSpec((None, None, s, d), lambda bi, gi, i: (bi, gi, 0, 0)),
            pl.BlockSpec((None, r * tq, 1), lambda bi, gi, i: (gi, 0, 0)),
            pl.BlockSpec(memory_space=pl.ANY),
        ],
        out_specs=pl.BlockSpec((None, None, r, tq, d), lambda bi, gi, i: (bi, gi, 0, i, 0)),
        out_shape=jax.ShapeDtypeStruct((b, g, r, s, d), MXU_DTYPE),
        input_output_aliases={4: 0},
        compiler_params=_cparams(("arbitrary", "arbitrary", "arbitrary")),
        name="window_attention",
    )(q5, k, v, sink_col, jnp.zeros((b, g, r, s, d), MXU_DTYPE))
    return o.reshape(b, hq, s, d)


def _na_tables(rpb, rows):
    w = GRID_W
    kh = min(NA_ROWS, rows)
    configs = [(0, 0), (NA_QROWS, NA_QROWS - NA_ROWS // 2), (rows - NA_QROWS, rows - NA_KROWS)]
    qi = jnp.arange(NA_QROWS, dtype=jnp.int32)[:, None, None, None]
    qc = jnp.arange(w, dtype=jnp.int32)[None, :, None, None]
    kj = jnp.arange(NA_KROWS, dtype=jnp.int32)[None, None, :, None]
    kc = jnp.arange(w, dtype=jnp.int32)[None, None, None, :]
    full = (NA_QROWS, w, NA_KROWS, w)
    flat = (NA_QROWS * w, NA_KROWS * w)
    tabs = []
    for r0, kr0 in configs:
        qr = r0 + qi
        kr = kr0 + kj
        rs = jnp.clip(qr - kh // 2, 0, rows - kh)
        cs = jnp.clip(qc - NA_COLS // 2, 0, w - NA_COLS)
        valid = (kr >= rs) & (kr < rs + kh) & (kc >= cs) & (kc < cs + NA_COLS)
        dr = jnp.clip(kr - qr + NA_ROWS - 1, 0, 2 * NA_ROWS - 2)
        dc = jnp.clip(kc - qc + NA_COLS - 1, 0, 2 * NA_COLS - 2)
        valid = jnp.broadcast_to(valid, full).reshape(flat)
        dr = jnp.broadcast_to(dr, full).reshape(flat)
        dc = jnp.broadcast_to(dc, full).reshape(flat)
        bias = rpb.astype(jnp.float32)[:, dr, dc]
        tabs.append(jnp.where(valid[None], bias, NEG_INF))
    return jnp.stack(tabs, axis=1)


def _na_kernel(q_ref, k_ref, v_ref, tab_ref, init_ref, o_ref, *, tq, tkw, n_lat, n_ctx, nq):
    del init_ref
    i = pl.program_id(2)
    kind = jnp.where(i == 0, 0, jnp.where(i == nq - 1, 2, 1))
    start = pl.multiple_of(jnp.clip(i * tq - (NA_ROWS // 2) * GRID_W, 0, n_lat - tkw), GRID_W)
    q = q_ref[...]
    k_loc = k_ref[pl.ds(start, tkw), :]
    v_loc = v_ref[pl.ds(start, tkw), :]
    k_ctx = k_ref[n_lat:n_lat + n_ctx, :]
    v_ctx = v_ref[n_lat:n_lat + n_ctx, :]
    s_loc = _dot_nt(q, k_loc) + tab_ref[kind]
    s_ctx = _dot_nt(q, k_ctx)
    m = jnp.maximum(jnp.max(s_loc, axis=-1, keepdims=True), jnp.max(s_ctx, axis=-1, keepdims=True))
    p_loc = jnp.exp(s_loc - m)
    p_ctx = jnp.exp(s_ctx - m)
    denom = jnp.sum(p_loc, axis=-1, keepdims=True) + jnp.sum(p_ctx, axis=-1, keepdims=True)
    o = (_dot(p_loc.astype(MXU_DTYPE), v_loc) + _dot(p_ctx.astype(MXU_DTYPE), v_ctx)) / denom
    o_ref[...] = o.astype(o_ref.dtype)


def _neighbourhood_attention(q, k, v, tables, n_lat, n_ctx):
    b, hq, s, d = q.shape
    tq = NA_QROWS * GRID_W
    tkw = NA_KROWS * GRID_W
    nq = n_lat // tq
    return pl.pallas_call(
        functools.partial(_na_kernel, tq=tq, tkw=tkw, n_lat=n_lat, n_ctx=n_ctx, nq=nq),
        grid=(b, hq, nq),
        in_specs=[
            pl.BlockSpec((None, None, tq, d), lambda bi, h, i: (bi, h, i, 0)),
            pl.BlockSpec((None, None, s, d), lambda bi, h, i: (bi, h, 0, 0)),
            pl.BlockSpec((None, None, s, d), lambda bi, h, i: (bi, h, 0, 0)),
            pl.BlockSpec((None, 3, tq, tkw), lambda bi, h, i: (h, 0, 0, 0)),
            pl.BlockSpec(memory_space=pl.ANY),
        ],
        out_specs=pl.BlockSpec((None, None, tq, d), lambda bi, h, i: (bi, h, i, 0)),
        out_shape=jax.ShapeDtypeStruct((b, hq, s, d), MXU_DTYPE),
        input_output_aliases={4: 0},
        compiler_params=_cparams(("arbitrary", "arbitrary", "arbitrary")),
        name="neighbourhood_attention",
    )(q, k, v, tables, jnp.zeros((b, hq, s, d), MXU_DTYPE))


def _outproj_kernel(o1_ref, o2_ref, w_ref, x_ref, mod_ref, rw_ref, rb_ref,
                    xo_ref, h_ref, lg_ref, *, tm, n_lat, n_heads):
    row0 = pl.program_id(1) * tm
    y = jnp.concatenate([o1_ref[h] for h in range(n_heads)] + [o2_ref[h] for h in range(n_heads)], axis=-1)
    z = _dot(y, w_ref[...])
    rows = row0 + lax.broadcasted_iota(jnp.int32, (tm, 1), 0)
    gate = jnp.where(rows >= n_lat, mod_ref[6:7, :], mod_ref[4:5, :])
    xn = x_ref[...] + gate * z
    xo_ref[...] = xn
    h = _modulated_norm(xn, mod_ref, row0, n_lat, 0, 1, 2, 3)
    h_ref[...] = h
    lg_ref[...] = jnp.dot(h, rw_ref[...], preferred_element_type=jnp.float32,
                          precision=lax.Precision.HIGHEST) + rb_ref[...]


def _outproj(o1, o2, w_out, xs, modv, router_w, router_b, n_lat):
    b, s, dm = xs.shape
    tm = ROW_TILE
    nh, dv = o1.shape[1], o1.shape[3]
    nl = router_w.shape[1]
    return pl.pallas_call(
        functools.partial(_outproj_kernel, tm=tm, n_lat=n_lat, n_heads=nh),
        grid=(b, s // tm),
        in_specs=[
            pl.BlockSpec((None, nh, tm, dv), lambda bi, j: (bi, 0, j, 0)),
            pl.BlockSpec((None, nh, tm, dv), lambda bi, j: (bi, 0, j, 0)),
            pl.BlockSpec(w_out.shape, lambda bi, j: (0, 0)),
            pl.BlockSpec((None, tm, dm), lambda bi, j: (bi, j, 0)),
            pl.BlockSpec((None, 8, dm), lambda bi, j: (bi, 0, 0)),
            pl.BlockSpec(router_w.shape, lambda bi, j: (0, 0)),
            pl.BlockSpec(router_b.shape, lambda bi, j: (0, 0)),
        ],
        out_specs=[pl.BlockSpec((None, tm, dm), lambda bi, j: (bi, j, 0)),
                   pl.BlockSpec((None, tm, dm), lambda bi, j: (bi, j, 0)),
                   pl.BlockSpec((None, tm, nl), lambda bi, j: (bi, j, 0))],
        out_shape=[jax.ShapeDtypeStruct((b, s, dm), jnp.float32),
                   jax.ShapeDtypeStruct((b, s, dm), jnp.float32),
                   jax.ShapeDtypeStruct((b, s, nl), jnp.float32)],
        input_output_aliases={3: 0},
        compiler_params=_cparams(("arbitrary", "arbitrary")),
        name="outproj_residual_norm_router",
    )(o1, o2, w_out, xs, modv, router_w, router_b)


def _expert_kernel(be_ref, nu_ref, tok0_ref, tokn_ref, h_hbm, win_ref, bin_ref, wout_ref, bout_ref, gate_ref,
                   y_ref, xbuf, sems, *, blk, nb):
    i = pl.program_id(0)
    slot = lax.rem(i, 2)

    def gather(tok_ref, dst_slot):
        for rr in range(blk):
            pltpu.make_async_copy(h_hbm.at[pl.ds(tok_ref[0, rr], 1)], xbuf.at[dst_slot, pl.ds(rr, 1)],
                                  sems.at[dst_slot]).start()

    @pl.when(i == 0)
    def _():
        gather(tok0_ref, 0)

    @pl.when(i + 1 < nb)
    def _():
        gather(tokn_ref, 1 - slot)

    pltpu.make_async_copy(h_hbm.at[pl.ds(0, blk)], xbuf.at[slot], sems.at[slot]).wait()

    @pl.when(i < nu_ref[0])
    def _():
        x = xbuf[slot].astype(MXU_DTYPE)
        u = _dot(x, win_ref[...]) + bin_ref[...]
        glu = jnp.minimum(u[:, :D_EXPERT], SWIGLU_LIMIT)
        lin = jnp.clip(u[:, D_EXPERT:], -SWIGLU_LIMIT, SWIGLU_LIMIT)
        act = glu * jax.nn.sigmoid(SWIGLU_ALPHA * glu) * (lin + 1.0)
        y = _dot(act.astype(MXU_DTYPE), wout_ref[...]) + bout_ref[...]
        y_ref[...] = y * gate_ref[...]

    @pl.when(i >= nu_ref[0])
    def _():
        y_ref[...] = jnp.zeros_like(y_ref)


def _experts(h_flat, blk_expert, n_used, tok_buf, gate_buf, w_in, b_in, w_out, b_out):
    n, dm = h_flat.shape
    blk = MOE_ROWS
    nb = blk_expert.shape[0]
    de2 = w_in.shape[-1]
    tok3 = tok_buf.reshape(nb, 1, blk)
    grid_spec = pltpu.PrefetchScalarGridSpec(
        num_scalar_prefetch=2,
        grid=(nb,),
        in_specs=[
            pl.BlockSpec((None, 1, blk), lambda i, be, nu: (0, 0, 0), memory_space=pltpu.SMEM),
            pl.BlockSpec((None, 1, blk), lambda i, be, nu: (jnp.minimum(i + 1, nb - 1), 0, 0),
                         memory_space=pltpu.SMEM),
            pl.BlockSpec(memory_space=pl.ANY),
            pl.BlockSpec((None, dm, de2), lambda i, be, nu: (be[i], 0, 0)),
            pl.BlockSpec((None, 1, de2), lambda i, be, nu: (be[i], 0, 0)),
            pl.BlockSpec((None, de2 // 2, dm), lambda i, be, nu: (be[i], 0, 0)),
            pl.BlockSpec((None, 1, dm), lambda i, be, nu: (be[i], 0, 0)),
            pl.BlockSpec((blk, 1), lambda i, be, nu: (i, 0)),
        ],
        out_specs=pl.BlockSpec((blk, dm), lambda i, be, nu: (i, 0)),
        scratch_shapes=[pltpu.VMEM((2, blk, dm), jnp.float32), pltpu.SemaphoreType.DMA((2,))],
    )
    return pl.pallas_call(
        functools.partial(_expert_kernel, blk=blk, nb=nb),
        grid_spec=grid_spec,
        out_shape=jax.ShapeDtypeStruct((nb * blk, dm), jnp.float32),
        compiler_params=_cparams(("arbitrary",)),
        name="routed_experts",
    )(blk_expert, n_used, tok3, tok3, h_flat, w_in, b_in.reshape(N_EXPERTS, 1, de2), w_out,
      b_out.reshape(N_EXPERTS, 1, dm), gate_buf.reshape(nb * blk, 1))


def _combine_kernel(pos0_ref, posn_ref, y_hbm, x_ref, mod_ref, o_ref, gbuf, sems, *, tc, nsteps, n_lat, steps_per_batch):
    i = pl.program_id(0)
    slot = lax.rem(i, 2)

    def gather(pos_ref, dst_slot):
        def issue(t, carry):
            for kk in range(TOP_K):
                pltpu.make_async_copy(y_hbm.at[pl.ds(pos_ref[0, t * TOP_K + kk], 1)],
                                      gbuf.at[dst_slot, kk, pl.ds(t, 1)], sems.at[dst_slot]).start()
            return carry
        lax.fori_loop(0, tc, issue, 0, unroll=8)

    @pl.when(i == 0)
    def _():
        gather(pos0_ref, 0)

    @pl.when(i + 1 < nsteps)
    def _():
        gather(posn_ref, 1 - slot)

    for kk in range(TOP_K):
        pltpu.make_async_copy(y_hbm.at[pl.ds(0, tc)], gbuf.at[slot, kk], sems.at[slot]).wait()

    y = gbuf[slot, 0] + gbuf[slot, 1] + gbuf[slot, 2] + gbuf[slot, 3]
    row0 = lax.rem(i, steps_per_batch) * tc
    rows = row0 + lax.broadcasted_iota(jnp.int32, (tc, 1), 0)
    gate = jnp.where(rows >= n_lat, mod_ref[7:8, :], mod_ref[5:6, :])
    o_ref[...] = x_ref[...] + gate * y


def _combine(y_sorted, pos, xs_flat, modv, s, n_lat):
    n, dm = xs_flat.shape
    tc = COMBINE_ROWS
    nsteps = n // tc
    spb = s // tc
    pos3 = pos.reshape(nsteps, 1, tc * TOP_K)
    return pl.pallas_call(
        functools.partial(_combine_kernel, tc=tc, nsteps=nsteps, n_lat=n_lat, steps_per_batch=spb),
        grid=(nsteps,),
        in_specs=[
            pl.BlockSpec((None, 1, tc * TOP_K), lambda i: (0, 0, 0), memory_space=pltpu.SMEM),
            pl.BlockSpec((None, 1, tc * TOP_K), lambda i: (jnp.minimum(i + 1, nsteps - 1), 0, 0),
                         memory_space=pltpu.SMEM),
            pl.BlockSpec(memory_space=pl.ANY),
            pl.BlockSpec((tc, dm), lambda i: (i, 0)),
            pl.BlockSpec((None, 8, dm), lambda i: (i // spb, 0, 0)),
        ],
        out_specs=pl.BlockSpec((tc, dm), lambda i: (i, 0)),
        out_shape=jax.ShapeDtypeStruct((n, dm), jnp.float32),
        scratch_shapes=[pltpu.VMEM((2, TOP_K, tc, dm), jnp.float32), pltpu.SemaphoreType.DMA((2,))],
        input_output_aliases={3: 0},
        compiler_params=_cparams(("arbitrary",)),
        name="expert_combine",
    )(pos3, pos3, y_sorted, xs_flat, modv)


def _final_norm_kernel(x_ref, g_ref, o_ref):
    x = x_ref[...]
    o_ref[...] = x * lax.rsqrt(jnp.mean(x * x, axis=-1, keepdims=True) + NORM_EPS) * g_ref[...]


def _final_norm(xs, g, n_lat):
    b, s, dm = xs.shape
    tm = 512
    return pl.pallas_call(
        _final_norm_kernel,
        grid=(b, n_lat // tm),
        in_specs=[pl.BlockSpec((None, tm, dm), lambda bi, j: (bi, j, 0)),
                  pl.BlockSpec((1, dm), lambda bi, j: (0, 0))],
        out_specs=pl.BlockSpec((None, tm, dm), lambda bi, j: (bi, j, 0)),
        out_shape=jax.ShapeDtypeStruct((b, n_lat, dm), jnp.float32),
        compiler_params=_cparams(("arbitrary", "arbitrary")),
        name="final_norm",
    )(xs, g.reshape(1, dm))


def _swap_perm(width, dim):
    j = np.arange(width)
    half = dim // 2
    return np.where((j % dim) < half, j + half, j - half)


def _rope_tables(n_lat, n_ctx, dim):
    t = jnp.arange(n_lat, dtype=jnp.int32)
    row = (t // GRID_W).astype(jnp.float32)
    col = (t % GRID_W).astype(jnp.float32)
    quarter = dim // 4
    inv_freq = ROPE_BASE ** (-jnp.arange(quarter, dtype=jnp.float32) / quarter)
    ang = jnp.concatenate([row[:, None] * inv_freq, col[:, None] * inv_freq], axis=-1)
    cos, sin = jnp.cos(ang), jnp.sin(ang)
    reps = LANES // dim
    cos_t = jnp.tile(jnp.concatenate([cos, cos], axis=-1), (1, reps))
    sin_t = jnp.tile(jnp.concatenate([-sin, sin], axis=-1), (1, reps))
    cos_t = jnp.concatenate([cos_t, jnp.ones((n_ctx, LANES), jnp.float32)], axis=0)
    sin_t = jnp.concatenate([sin_t, jnp.zeros((n_ctx, LANES), jnp.float32)], axis=0)
    return cos_t, sin_t


def _mod_rows(mod_l, norm_g, b, first):
    lat, ctx = mod_l[:b], mod_l[b]
    a_lat = norm_g[None] * (1.0 + lat[:, first + 1])
    b_lat = lat[:, first]
    a_ctx = jnp.broadcast_to(norm_g * (1.0 + ctx[first + 1]), a_lat.shape)
    b_ctx = jnp.broadcast_to(ctx[first], a_lat.shape)
    g_mix_ctx = jnp.broadcast_to(ctx[2], a_lat.shape)
    g_ffn_ctx = jnp.broadcast_to(ctx[5], a_lat.shape)
    return jnp.stack([a_lat, b_lat, a_ctx, b_ctx, lat[:, 2], lat[:, 5], g_mix_ctx, g_ffn_ctx], axis=1)


def _route(logits, blk):
    n = logits.shape[0]
    nk = n * TOP_K
    top_val, top_idx = lax.top_k(logits, TOP_K)
    gate = jax.nn.softmax(top_val, axis=-1)
    expert = top_idx.reshape(-1)
    onehot = (expert[:, None] == jnp.arange(N_EXPERTS, dtype=jnp.int32)[None, :]).astype(jnp.int32)
    csum = jnp.cumsum(onehot, axis=0)
    rank = jnp.take_along_axis(csum, expert[:, None], axis=1)[:, 0] - 1
    sizes = csum[-1]
    padded = (sizes + blk - 1) // blk * blk
    pends = jnp.cumsum(padded)
    pstarts = pends - padded
    dest = pstarts[expert] + rank
    nb = -(-(nk + N_EXPERTS * (blk - 1)) // blk)
    token = jnp.arange(nk, dtype=jnp.int32) // TOP_K
    tok_buf = jnp.zeros((nb * blk,), jnp.int32).at[dest].set(token)
    gate_buf = jnp.zeros((nb * blk,), jnp.float32).at[dest].set(gate.reshape(-1))
    blk_expert = jnp.minimum(jnp.searchsorted(pends, jnp.arange(nb, dtype=jnp.int32) * blk, side='right'),
                             N_EXPERTS - 1).astype(jnp.int32)
    n_used = (pends[-1] // blk).astype(jnp.int32).reshape(1)
    return blk_expert, n_used, tok_buf, gate_buf, dest.astype(jnp.int32)


def kernel(x, c, ctx, c_ctx, mod_w, mod_b, norm_mix, norm_ffn, ab_w_in, ab_w_out, a_sink, b_rpb,
           cd_w_in, c_q_norm, c_w_q_b, c_kv_norm, c_w_kv_b, d_q_norm, d_k_norm, cd_w_out,
           router_w, router_b, exp_w_in, exp_b_in, exp_w_out, exp_b_out, final_norm):
    b, t, dm = x.shape
    n_ctx = ctx.shape[1]
    s = t + n_ctx
    depth = mod_w.shape[0]
    rows = t // GRID_W
    d = HEAD_DIM
    f32 = jnp.float32

    xs = jnp.concatenate([x, ctx], axis=1)

    cond = jnp.zeros((8, dm), f32).at[:b].set(c).at[b].set(c_ctx)
    mod_all = _modulation(cond, mod_w, mod_b)[:, :b + 1].reshape(depth, b + 1, 6, dm)

    cos_t, sin_t = _rope_tables(t, n_ctx, d)
    cosm_t, sinm_t = _rope_tables(t, n_ctx, C_ROPE)

    router_w_p = jnp.zeros((depth, dm, LANES), f32).at[:, :, :N_EXPERTS].set(router_w)
    router_b_p = jnp.zeros((depth, 1, LANES), f32).at[:, 0, :N_EXPERTS].set(router_b)

    p64_512 = _swap_perm(512, d)
    p64_128 = _swap_perm(128, d)
    p32 = _swap_perm(C_ROPE, C_ROPE)

    for layer in range(depth):
        i = layer // 2
        modv_mix = _mod_rows(mod_all[layer], norm_mix[layer], b, 0)
        modv_ffn = _mod_rows(mod_all[layer], norm_ffn[layer], b, 3)
        if layer % 2 == 0:
            w = ab_w_in[i]
            w_ext = jnp.concatenate([w, w[:, 0:512][:, p64_512], w[:, 512:640][:, p64_128]], axis=1).astype(MXU_DTYPE)
            qa, ka, va, qb, kb, vb = _proj_ab(xs, modv_mix, w_ext, cos_t, sin_t, t)
            sink_gr = a_sink[i].astype(f32).reshape(A_KV_HEADS, A_HEADS // A_KV_HEADS)
            r = A_HEADS // A_KV_HEADS
            sink_win = jnp.repeat(sink_gr, WIN_Q, axis=1).reshape(A_KV_HEADS, r * WIN_Q, 1)
            sink_ctx = jnp.repeat(sink_gr, n_ctx, axis=1).reshape(A_KV_HEADS, r * n_ctx, 1)
            o1 = _window_attention(qa, ka, va, sink_win, t, n_ctx)
            o1 = _flash(qa, ka, va, groups=A_KV_HEADS, tq=n_ctx, tk=n_ctx, q_row0=t, n_q=n_ctx,
                        kv_row0=t, n_kv=n_ctx, sink=sink_ctx, out=o1)
            tables = _na_tables(b_rpb[i], rows)
            o2 = _neighbourhood_attention(qb, kb, vb, tables, t, n_ctx)
            o2 = _flash(qb, kb, vb, groups=B_HEADS, tq=n_ctx, tk=n_ctx, q_row0=t, n_q=n_ctx,
                        kv_row0=t, n_kv=n_ctx, out=o2)
            w_out = ab_w_out[i].astype(MXU_DTYPE)
        else:
            w = cd_w_in[i]
            base = C_Q_RANK + C_KV_RANK
            o_qd, o_kd, o_vd = base + C_ROPE, base + C_ROPE + 512, base + C_ROPE + 640
            wqd, wkd = w[:, o_qd:o_qd + 512], w[:, o_kd:o_kd + 128]
            gq = jnp.tile(d_q_norm[i].astype(f32), D_HEADS)
            gk = jnp.tile(d_k_norm[i].astype(f32), D_KV_HEADS)
            krope = w[:, base:base + C_ROPE]
            pad = jnp.zeros((dm, LANES - C_ROPE), f32)
            w1 = jnp.concatenate([
                w[:, 0:base], wqd, wkd, w[:, o_vd:o_vd + 128],
                (wqd * gq[None])[:, p64_512], (wkd * gk[None])[:, p64_128],
                krope, pad, krope[:, p32], pad], axis=1).astype(MXU_DTYPE)
            dq = C_NOPE + C_ROPE
            wq = c_w_q_b[i].reshape(C_Q_RANK, C_HEADS, dq)
            wq_nope = wq[:, :, :C_NOPE].reshape(C_Q_RANK, C_HEADS * C_NOPE)
            wq_rope = wq[:, :, C_NOPE:].reshape(C_Q_RANK, C_HEADS * C_ROPE)
            wq2 = jnp.concatenate([wq_nope, wq_rope, wq_rope[:, _swap_perm(C_HEADS * C_ROPE, C_ROPE)]],
                                  axis=1).astype(MXU_DTYPE)
            wkv = c_w_kv_b[i].reshape(C_KV_RANK, C_HEADS, C_NOPE + C_V)
            wkv2 = jnp.concatenate([wkv[:, :, :C_NOPE].reshape(C_KV_RANK, -1),
                                    wkv[:, :, C_NOPE:].reshape(C_KV_RANK, -1)], axis=1).astype(MXU_DTYPE)
            qc, kc, vc, qd, kd, vd = _proj_cd(
                xs, modv_mix, w1, wq2, wkv2, c_q_norm[i].astype(f32).reshape(1, -1),
                c_kv_norm[i].astype(f32).reshape(1, -1), gq.reshape(1, -1), gk.reshape(1, -1),
                cos_t, sin_t, cosm_t, sinm_t, t)
            o1 = _flash(qc, kc, vc, groups=C_HEADS, tq=FLASH_M, tk=FLASH_TK, q_row0=0, n_q=t, kv_row0=0, n_kv=s,
                        out=jnp.zeros((b, C_HEADS, s, C_V), MXU_DTYPE))
            o1 = _flash(qc, kc, vc, groups=C_HEADS, tq=n_ctx, tk=n_ctx, q_row0=t, n_q=n_ctx,
                        kv_row0=t, n_kv=n_ctx, out=o1)
            rd = D_HEADS // D_KV_HEADS
            o2 = _flash(qd, kd, vd, groups=D_KV_HEADS, tq=FLASH_M // rd, tk=FLASH_TK, q_row0=0, n_q=t,
                        kv_row0=0, n_kv=s, out=jnp.zeros((b, D_HEADS, s, d), MXU_DTYPE))
            o2 = _flash(qd, kd, vd, groups=D_KV_HEADS, tq=n_ctx, tk=n_ctx, q_row0=t, n_q=n_ctx,
                        kv_row0=t, n_kv=n_ctx, out=o2)
            w_out = cd_w_out[i].astype(MXU_DTYPE)

        xs, h2, logits = _outproj(o1, o2, w_out, xs, modv_ffn, router_w_p[layer], router_b_p[layer], t)

        n = b * s
        blk_expert, n_used, tok_buf, gate_buf, pos = _route(logits.reshape(n, LANES)[:, :N_EXPERTS], MOE_ROWS)
        y_sorted = _experts(h2.reshape(n, dm), blk_expert, n_used, tok_buf, gate_buf,
                            exp_w_in[layer].astype(MXU_DTYPE), exp_b_in[layer].astype(f32),
                            exp_w_out[layer].astype(MXU_DTYPE), exp_b_out[layer].astype(f32))
        xs = _combine(y_sorted, pos, xs.reshape(n, dm), modv_ffn, s, t).reshape(b, s, dm)

    return _final_norm(xs, final_norm.astype(f32), t)
```

```python
import functools

import jax
import jax.numpy as jnp
import numpy as np
from jax import lax
from jax.experimental import pallas as pl
from jax.experimental.pallas import tpu as pltpu

GRID_W = 64
HEAD_DIM = 64
ROPE_BASE = 10000.0
NORM_EPS = 1e-6
NEG_INF = -1e30
A_HEADS = 8
A_KV_HEADS = 2
A_WINDOW = 128
B_HEADS = 8
NA_ROWS = 8
NA_COLS = 16
C_HEADS = 8
C_Q_RANK = 768
C_KV_RANK = 256
C_NOPE = 64
C_ROPE = 32
C_V = 64
D_HEADS = 8
D_KV_HEADS = 2
N_EXPERTS = 32
TOP_K = 4
D_EXPERT = 1024
SWIGLU_LIMIT = 7.0
SWIGLU_ALPHA = 1.702
LOG2_E = 1.4426950408889634

LANES = 128
VMEM_LIMIT_BYTES = 56 * 1024 * 1024

MXU_DTYPE = jnp.bfloat16

ROW_TILE = 640
MOE_ROWS = 256
COMBINE_ROWS = 256
NA_QROWS = 8
NA_KROWS = 16
WIN_Q = 256
FLASH_M = 1024
FLASH_TK = 1280


def _cparams(sem, vmem=VMEM_LIMIT_BYTES):
    return pltpu.CompilerParams(dimension_semantics=sem, vmem_limit_bytes=vmem)


def _dot(a, b):
    return jnp.dot(a, b, preferred_element_type=jnp.float32)


def _dot_nt(a, b):
    return lax.dot_general(a, b, (((1,), (1,)), ((), ())), preferred_element_type=jnp.float32)


def _mod_kernel(c_ref, w_ref, b_ref, o_ref):
    c = c_ref[...]
    s = c * jax.nn.sigmoid(c)
    o_ref[...] = jnp.dot(s, w_ref[...], preferred_element_type=jnp.float32,
                         precision=lax.Precision.HIGHEST) + b_ref[...]


def _modulation(cond, mod_w, mod_b):
    depth, dm, n = mod_w.shape
    tn = 1536
    return pl.pallas_call(
        _mod_kernel,
        grid=(depth, n // tn),
        in_specs=[
            pl.BlockSpec((8, dm), lambda l, j: (0, 0)),
            pl.BlockSpec((None, dm, tn), lambda l, j: (l, 0, j)),
            pl.BlockSpec((None, 1, tn), lambda l, j: (l, 0, j)),
        ],
        out_specs=pl.BlockSpec((None, 8, tn), lambda l, j: (l, 0, j)),
        out_shape=jax.ShapeDtypeStruct((depth, 8, n), jnp.float32),
        compiler_params=_cparams(("arbitrary", "arbitrary")),
        name="adaln_modulation",
    )(cond, mod_w, mod_b.reshape(depth, 1, n))


def _modulated_norm(x, mod_ref, row0, n_lat, a_lat, b_lat, a_ctx, b_ctx):
    tm = x.shape[0]
    ms = jnp.mean(x * x, axis=-1, keepdims=True)
    xh = x * lax.rsqrt(ms + NORM_EPS)
    rows = row0 + lax.broadcasted_iota(jnp.int32, (tm, 1), 0)
    is_ctx = rows >= n_lat
    a = jnp.where(is_ctx, mod_ref[a_ctx:a_ctx + 1, :], mod_ref[a_lat:a_lat + 1, :])
    b = jnp.where(is_ctx, mod_ref[b_ctx:b_ctx + 1, :], mod_ref[b_lat:b_lat + 1, :])
    return xh * a + b


def _store_heads(o_ref, val, n_heads, width, lane0=0):
    for h in range(n_heads):
        o_ref[h, :, lane0:lane0 + width] = val[:, h * width:(h + 1) * width].astype(o_ref.dtype)


def _tile_lanes(t, reps):
    return t if reps == 1 else jnp.concatenate([t] * reps, axis=-1)


def _proj_ab_kernel(x_ref, mod_ref, w_ref, cos_ref, sin_ref,
                    qa_ref, ka_ref, va_ref, qb_ref, kb_ref, vb_ref, *, tm, n_lat):
    row0 = pl.program_id(1) * tm
    h = _modulated_norm(x_ref[...], mod_ref, row0, n_lat, 0, 1, 2, 3).astype(MXU_DTYPE)
    cos = cos_ref[...]
    sin = sin_ref[...]
    d = HEAD_DIM
    scale = d ** -0.5
    qa = _dot(h, w_ref[:, 0:512]) * _tile_lanes(cos, 4) + _dot(h, w_ref[:, 2304:2816]) * _tile_lanes(sin, 4)
    _store_heads(qa_ref, qa * scale, A_HEADS, d)
    ka = _dot(h, w_ref[:, 512:640]) * cos + _dot(h, w_ref[:, 2816:2944]) * sin
    _store_heads(ka_ref, ka, A_KV_HEADS, d)
    _store_heads(va_ref, _dot(h, w_ref[:, 640:768]), A_KV_HEADS, d)
    _store_heads(qb_ref, _dot(h, w_ref[:, 768:1280]) * scale, B_HEADS, d)
    _store_heads(kb_ref, _dot(h, w_ref[:, 1280:1792]), B_HEADS, d)
    _store_heads(vb_ref, _dot(h, w_ref[:, 1792:2304]), B_HEADS, d)


def _proj_ab(xs, modv, w_ext, cos_t, sin_t, n_lat):
    b, s, dm = xs.shape
    tm = ROW_TILE
    d = HEAD_DIM
    nw = w_ext.shape[1]

    def head_spec(nh):
        return pl.BlockSpec((None, nh, tm, d), lambda bi, j: (bi, 0, j, 0))

    def head_shape(nh):
        return jax.ShapeDtypeStruct((b, nh, s, d), MXU_DTYPE)

    return pl.pallas_call(
        functools.partial(_proj_ab_kernel, tm=tm, n_lat=n_lat),
        grid=(b, s // tm),
        in_specs=[
            pl.BlockSpec((None, tm, dm), lambda bi, j: (bi, j, 0)),
            pl.BlockSpec((None, 8, dm), lambda bi, j: (bi, 0, 0)),
            pl.BlockSpec((dm, nw), lambda bi, j: (0, 0)),
            pl.BlockSpec((tm, LANES), lambda bi, j: (j, 0)),
            pl.BlockSpec((tm, LANES), lambda bi, j: (j, 0)),
        ],
        out_specs=[head_spec(A_HEADS), head_spec(A_KV_HEADS), head_spec(A_KV_HEADS),
                   head_spec(B_HEADS), head_spec(B_HEADS), head_spec(B_HEADS)],
        out_shape=[head_shape(A_HEADS), head_shape(A_KV_HEADS), head_shape(A_KV_HEADS),
                   head_shape(B_HEADS), head_shape(B_HEADS), head_shape(B_HEADS)],
        compiler_params=_cparams(("arbitrary", "arbitrary")),
        name="proj_ab",
    )(xs, modv, w_ext, cos_t, sin_t)


def _proj_cd_kernel(x_ref, mod_ref, w1_ref, wq2_ref, wkv2_ref, qn_ref, kvn_ref, gq_ref, gk_ref,
                    cos_ref, sin_ref, cosm_ref, sinm_ref,
                    qc_ref, kc_ref, vc_ref, qd_ref, kd_ref, vd_ref, *, tm, n_lat):
    row0 = pl.program_id(1) * tm
    h = _modulated_norm(x_ref[...], mod_ref, row0, n_lat, 0, 1, 2, 3).astype(MXU_DTYPE)
    cos = cos_ref[...]
    sin = sin_ref[...]
    cosm = cosm_ref[...]
    sinm = sinm_ref[...]
    d = HEAD_DIM

    cq = _dot(h, w1_ref[:, 0:768])
    cqn = cq * lax.rsqrt(jnp.mean(cq * cq, axis=-1, keepdims=True) + NORM_EPS) * qn_ref[...]
    q2 = _dot(cqn.astype(MXU_DTYPE), wq2_ref[...])
    c_scale = (C_NOPE + C_ROPE) ** -0.5 * LOG2_E
    q_rope = (q2[:, 512:768] * _tile_lanes(cosm, 2) + q2[:, 768:1024] * _tile_lanes(sinm, 2)) * c_scale
    _store_heads(qc_ref, q2[:, 0:512] * c_scale, C_HEADS, C_NOPE)
    _store_heads(qc_ref, q_rope, C_HEADS, C_ROPE, lane0=C_NOPE)

    ckv = _dot(h, w1_ref[:, 768:1024])
    ckvn = ckv * lax.rsqrt(jnp.mean(ckv * ckv, axis=-1, keepdims=True) + NORM_EPS) * kvn_ref[...]
    kv2 = _dot(ckvn.astype(MXU_DTYPE), wkv2_ref[...])
    k_rope = (_dot(h, w1_ref[:, 2432:2560]) * cosm + _dot(h, w1_ref[:, 2560:2688]) * sinm)[:, 0:C_ROPE]
    _store_heads(kc_ref, kv2[:, 0:512], C_HEADS, C_NOPE)
    for hh in range(C_HEADS):
        kc_ref[hh, :, C_NOPE:C_NOPE + C_ROPE] = k_rope.astype(kc_ref.dtype)
    _store_heads(vc_ref, kv2[:, 512:1024], C_HEADS, C_V)

    def normed_rope(raw, swapped_gained, gain, n_heads, o_ref, scale):
        reps = n_heads * d // LANES
        val = raw * (_tile_lanes(cos, reps) * gain) + swapped_gained * _tile_lanes(sin, reps)
        for hh in range(n_heads):
            sl = raw[:, hh * d:(hh + 1) * d]
            r = lax.rsqrt(jnp.mean(sl * sl, axis=-1, keepdims=True) + NORM_EPS)
            o_ref[hh, :, :] = (val[:, hh * d:(hh + 1) * d] * (r * scale)).astype(o_ref.dtype)

    normed_rope(_dot(h, w1_ref[:, 1024:1536]), _dot(h, w1_ref[:, 1792:2304]), gq_ref[...], D_HEADS, qd_ref,
                d ** -0.5 * LOG2_E)
    normed_rope(_dot(h, w1_ref[:, 1536:1664]), _dot(h, w1_ref[:, 2304:2432]), gk_ref[...], D_KV_HEADS, kd_ref, 1.0)
    _store_heads(vd_ref, _dot(h, w1_ref[:, 1664:1792]), D_KV_HEADS, d)


def _proj_cd(xs, modv, w1, wq2, wkv2, qn, kvn, gq, gk, cos_t, sin_t, cosm_t, sinm_t, n_lat):
    b, s, dm = xs.shape
    tm = ROW_TILE
    dk_c = C_NOPE + C_ROPE

    def full(a):
        return pl.BlockSpec(a.shape, lambda bi, j: (0,) * a.ndim)

    def head_spec(nh, w):
        return pl.BlockSpec((None, nh, tm, w), lambda bi, j: (bi, 0, j, 0))

    def head_shape(nh, w):
        return jax.ShapeDtypeStruct((b, nh, s, w), MXU_DTYPE)

    tab = pl.BlockSpec((tm, LANES), lambda bi, j: (j, 0))
    return pl.pallas_call(
        functools.partial(_proj_cd_kernel, tm=tm, n_lat=n_lat),
        grid=(b, s // tm),
        in_specs=[
            pl.BlockSpec((None, tm, dm), lambda bi, j: (bi, j, 0)),
            pl.BlockSpec((None, 8, dm), lambda bi, j: (bi, 0, 0)),
            full(w1), full(wq2), full(wkv2), full(qn), full(kvn), full(gq), full(gk),
            tab, tab, tab, tab,
        ],
        out_specs=[head_spec(C_HEADS, dk_c), head_spec(C_HEADS, dk_c), head_spec(C_HEADS, C_V),
                   head_spec(D_HEADS, HEAD_DIM), head_spec(D_KV_HEADS, HEAD_DIM), head_spec(D_KV_HEADS, HEAD_DIM)],
        out_shape=[head_shape(C_HEADS, dk_c), head_shape(C_HEADS, dk_c), head_shape(C_HEADS, C_V),
                   head_shape(D_HEADS, HEAD_DIM), head_shape(D_KV_HEADS, HEAD_DIM), head_shape(D_KV_HEADS, HEAD_DIM)],
        compiler_params=_cparams(("arbitrary", "arbitrary")),
        name="proj_cd",
    )(xs, modv, w1, wq2, wkv2, qn, kvn, gq, gk, cos_t, sin_t, cosm_t, sinm_t)


def _flash_kernel(*refs, r, tq, tk, nk, has_sink):
    if has_sink:
        q_ref, k_ref, v_ref, sink_ref, o_ref, m_sc, l_sc, acc_sc = refs
    else:
        q_ref, k_ref, v_ref, o_ref, m_sc, l_sc, acc_sc = refs
    m_rows = r * tq
    q = q_ref[...].reshape(m_rows, q_ref.shape[-1])
    if has_sink:
        m_sc[...] = sink_ref[...]
        l_sc[...] = jnp.ones_like(l_sc)
    else:
        m_sc[...] = jnp.full_like(m_sc, NEG_INF)
        l_sc[...] = jnp.zeros_like(l_sc)
    acc_sc[...] = jnp.zeros_like(acc_sc)

    def body(j, carry):
        k0 = pl.multiple_of(j * tk, tk)
        k = k_ref[pl.ds(k0, tk), :]
        v = v_ref[pl.ds(k0, tk), :]
        s = _dot_nt(q, k)
        m_prev = m_sc[...]
        m_new = jnp.maximum(m_prev, jnp.max(s, axis=-1, keepdims=True))
        alpha = jnp.exp(m_prev - m_new)
        p = jnp.exp(s - m_new)
        l_sc[...] = alpha * l_sc[...] + jnp.sum(p, axis=-1, keepdims=True)
        acc_sc[...] = alpha * acc_sc[...] + _dot(p.astype(MXU_DTYPE), v)
        m_sc[...] = m_new
        return carry

    lax.fori_loop(0, nk, body, 0)
    o = acc_sc[...] / l_sc[...]
    o_ref[...] = o.reshape(o_ref.shape).astype(o_ref.dtype)


def _flash(q, k, v, *, groups, tq, tk, q_row0, n_q, kv_row0, n_kv, sink=None, out=None):
    b, hq, s, dk = q.shape
    dv = v.shape[-1]
    r = hq // groups
    q5 = q.reshape(b, groups, r, s, dk)
    nk = n_kv // tk
    has_sink = sink is not None
    m_rows = r * tq
    in_specs = [
        pl.BlockSpec((None, None, r, tq, dk), lambda bi, g, i: (bi, g, 0, q_row0 // tq + i, 0)),
        pl.BlockSpec((None, None, n_kv, dk), lambda bi, g, i: (bi, g, kv_row0 // n_kv, 0)),
        pl.BlockSpec((None, None, n_kv, dv), lambda bi, g, i: (bi, g, kv_row0 // n_kv, 0)),
    ]
    args = [q5, k, v]
    if has_sink:
        in_specs.append(pl.BlockSpec((None, m_rows, 1), lambda bi, g, i: (g, 0, 0)))
        args.append(sink)
    aliases = {}
    if out is not None:
        in_specs.append(pl.BlockSpec(memory_space=pl.ANY))
        args.append(out.reshape(b, groups, r, s, dv))
        aliases = {len(args) - 1: 0}
    kern = functools.partial(_flash_kernel, r=r, tq=tq, tk=tk, nk=nk, has_sink=has_sink)
    if out is not None:
        inner = kern
        kern = lambda *refs: inner(*refs[:len(args) - 1], *refs[len(args):])
    o = pl.pallas_call(
        kern,
        grid=(b, groups, n_q // tq),
        in_specs=in_specs,
        out_specs=pl.BlockSpec((None, None, r, tq, dv), lambda bi, g, i: (bi, g, 0, q_row0 // tq + i, 0)),
        out_shape=jax.ShapeDtypeStruct((b, groups, r, s, dv), MXU_DTYPE),
        scratch_shapes=[pltpu.VMEM((m_rows, 1), jnp.float32), pltpu.VMEM((m_rows, 1), jnp.float32),
                        pltpu.VMEM((m_rows, dv), jnp.float32)],
        input_output_aliases=aliases,
        compiler_params=_cparams(("arbitrary", "arbitrary", "arbitrary")),
        name="dense_attention",
    )(*args)
    return o.reshape(b, hq, s, dv)


def _flash_t_kernel(q_ref, k_ref, vt_ref, init_ref, o_ref, m_sc, l_sc, acc_sc, sa_sc, sb_sc, *, r, tq, tk, nk):
    del init_ref
    m_cols = r * tq
    q = q_ref[...].reshape(m_cols, q_ref.shape[-1])
    m_sc[...] = jnp.full_like(m_sc, NEG_INF)
    l_sc[...] = jnp.zeros_like(l_sc)
    acc_sc[...] = jnp.zeros_like(acc_sc)

    def scores(j, s_ref):
        k0 = pl.multiple_of(j * tk, tk)
        s_ref[...] = _dot_nt(k_ref[pl.ds(k0, tk), :], q)

    def update(j, s_ref):
        st = s_ref[...]
        m_prev = m_sc[...]
        m_new = jnp.maximum(m_prev, jnp.max(st, axis=0, keepdims=True))
        alpha = jnp.exp2(m_prev - m_new)
        pt = jnp.exp2(st - m_new)
        l_sc[...] = alpha * l_sc[...] + jnp.sum(pt, axis=0, keepdims=True)
        acc_sc[...] = alpha * acc_sc[...] + _dot(vt_ref[j], pt.astype(MXU_DTYPE))
        m_sc[...] = m_new

    scores(0, sa_sc)

    def body(jj, carry):
        j = 2 * jj
        scores(j + 1, sb_sc)
        update(j, sa_sc)
        scores(j + 2, sa_sc)
        update(j + 1, sb_sc)
        return carry

    lax.fori_loop(0, (nk - 1) // 2, body, 0)
    if nk % 2 == 0:
        scores(nk - 1, sb_sc)
        update(nk - 2, sa_sc)
        update(nk - 1, sb_sc)
    else:
        update(nk - 1, sa_sc)
    o = acc_sc[...] / l_sc[...]
    for rr in range(r):
        o_ref[rr] = o[:, rr * tq:(rr + 1) * tq].astype(o_ref.dtype)


def _flash_t(q, k, v, *, groups, tq, tk, q_row0, n_q, kv_row0, n_kv, out):
    b, hq, s, dk = q.shape
    dv = v.shape[-1]
    r = hq // groups
    q5 = q.reshape(b, groups, r, s, dk)
    nk = n_kv // tk
    vt = jnp.swapaxes(lax.slice_in_dim(v, kv_row0, kv_row0 + n_kv, axis=2).reshape(b, groups, nk, tk, dv), 3, 4)
    m_cols = r * tq
    o = pl.pallas_call(
        functools.partial(_flash_t_kernel, r=r, tq=tq, tk=tk, nk=nk),
        grid=(b, groups, n_q // tq),
        in_specs=[
            pl.BlockSpec((None, None, r, tq, dk), lambda bi, g, i: (bi, g, 0, q_row0 // tq + i, 0)),
            pl.BlockSpec((None, None, n_kv, dk), lambda bi, g, i: (bi, g, kv_row0 // n_kv, 0)),
            pl.BlockSpec((None, None, nk, dv, tk), lambda bi, g, i: (bi, g, 0, 0, 0)),
            pl.BlockSpec(memory_space=pl.ANY),
        ],
        out_specs=pl.BlockSpec((None, None, r, dv, tq), lambda bi, g, i: (bi, g, 0, 0, q_row0 // tq + i)),
        out_shape=jax.ShapeDtypeStruct((b, groups, r, dv, s), MXU_DTYPE),
        scratch_shapes=[pltpu.VMEM((1, m_cols), jnp.float32), pltpu.VMEM((1, m_cols), jnp.float32),
                        pltpu.VMEM((dv, m_cols), jnp.float32),
                        pltpu.VMEM((tk, m_cols), jnp.float32), pltpu.VMEM((tk, m_cols), jnp.float32)],
        input_output_aliases={3: 0},
        compiler_params=_cparams(("arbitrary", "arbitrary", "arbitrary")),
        name="dense_attention_t",
    )(q5, k, vt, out.reshape(b, groups, r, dv, s))
    return o.reshape(b, hq, dv, s)


def _window_kernel(q_ref, k_ref, v_ref, sink_ref, init_ref, o_ref, *, r, tq, n_lat, n_ctx):
    del init_ref
    i = pl.program_id(2)
    span = tq + 2 * A_WINDOW
    d = q_ref.shape[-1]
    q = q_ref[...].reshape(r * tq, d)
    start = pl.multiple_of(jnp.clip(i * tq - A_WINDOW, 0, n_lat - span), A_WINDOW)
    k_loc = k_ref[pl.ds(start, span), :]
    v_loc = v_ref[pl.ds(start, span), :]
    k_ctx = k_ref[n_lat:n_lat + n_ctx, :]
    v_ctx = v_ref[n_lat:n_lat + n_ctx, :]
    qpos = i * tq + lax.rem(lax.broadcasted_iota(jnp.int32, (r * tq, span), 0), tq)
    kpos = start + lax.broadcasted_iota(jnp.int32, (r * tq, span), 1)
    ok = jnp.abs(qpos - kpos) <= A_WINDOW
    s_loc = jnp.where(ok, _dot_nt(q, k_loc), NEG_INF)
    s_ctx = _dot_nt(q, k_ctx)
    sink = sink_ref[...]
    m = jnp.maximum(jnp.maximum(jnp.max(s_loc, axis=-1, keepdims=True), jnp.max(s_ctx, axis=-1, keepdims=True)), sink)
    p_loc = jnp.exp(s_loc - m)
    p_ctx = jnp.exp(s_ctx - m)
    denom = jnp.sum(p_loc, axis=-1, keepdims=True) + jnp.sum(p_ctx, axis=-1, keepdims=True) + jnp.exp(sink - m)
    o = (_dot(p_loc.astype(MXU_DTYPE), v_loc) + _dot(p_ctx.astype(MXU_DTYPE), v_ctx)) / denom
    o_ref[...] = o.reshape(o_ref.shape).astype(o_ref.dtype)


def _window_attention(q, k, v, sink_col, n_lat, n_ctx):
    b, hq, s, d = q.shape
    g = k.shape[1]
    r = hq // g
    tq = WIN_Q
    q5 = q.reshape(b, g, r, s, d)
    o = pl.pallas_call(
        functools.partial(_window_kernel, r=r, tq=tq, n_lat=n_lat, n_ctx=n_ctx),
        grid=(b, g, n_lat // tq),
        in_specs=[
            pl.BlockSpec((None, None, r, tq, d), lambda bi, gi, i: (bi, gi, 0, i, 0)),
            pl.BlockSpec((None, None, s, d), lambda bi, gi, i: (bi, gi, 0, 0)),
            pl.BlockSpec((None, None, s, d), lambda bi, gi, i: (bi, gi, 0, 0)),
            pl.BlockSpec((None, r * tq, 1), lambda bi, gi, i: (gi, 0, 0)),
            pl.BlockSpec(memory_space=pl.ANY),
        ],
        out_specs=pl.BlockSpec((None, None, r, tq, d), lambda bi, gi, i: (bi, gi, 0, i, 0)),
        out_shape=jax.ShapeDtypeStruct((b, g, r, s, d), MXU_DTYPE),
        input_output_aliases={4: 0},
        compiler_params=_cparams(("arbitrary", "arbitrary", "arbitrary")),
        name="window_attention",
    )(q5, k, v, sink_col, jnp.zeros((b, g, r, s, d), MXU_DTYPE))
    return o.reshape(b, hq, s, d)


def _toeplitz(a, n_q, n_k, off):
    length = a.shape[-1]
    lo = (n_q - 1) - off
    hi = (n_q + n_k - 1) - lo - length
    ap = jnp.pad(a, [(0, 0)] * (a.ndim - 1) + [(max(lo, 0), max(hi, 0))])
    ap = ap[..., max(-lo, 0):ap.shape[-1] - max(-hi, 0)]
    return jnp.stack([ap[..., n_q - 1 - q:n_q - 1 - q + n_k] for q in range(n_q)], axis=-2)


def _na_tables(rpb, rows):
    w = GRID_W
    kh = min(NA_ROWS, rows)
    configs = [(0, 0), (NA_QROWS, NA_QROWS - NA_ROWS // 2), (rows - NA_QROWS, rows - NA_KROWS)]
    qi = jnp.arange(NA_QROWS, dtype=jnp.int32)[:, None, None, None]
    qc = jnp.arange(w, dtype=jnp.int32)[None, :, None, None]
    kj = jnp.arange(NA_KROWS, dtype=jnp.int32)[None, None, :, None]
    kc = jnp.arange(w, dtype=jnp.int32)[None, None, None, :]
    full = (NA_QROWS, w, NA_KROWS, w)
    flat = (NA_QROWS * w, NA_KROWS * w)
    by_col = jnp.moveaxis(_toeplitz(rpb.astype(jnp.float32), w, w, NA_COLS - 1), 1, -1)
    tabs = []
    for r0, kr0 in configs:
        qr = r0 + qi
        kr = kr0 + kj
        rs = jnp.clip(qr - kh // 2, 0, rows - kh)
        cs = jnp.clip(qc - NA_COLS // 2, 0, w - NA_COLS)
        valid = (kr >= rs) & (kr < rs + kh) & (kc >= cs) & (kc < cs + NA_COLS)
        valid = jnp.broadcast_to(valid, full).reshape(flat)
        bias = _toeplitz(by_col, NA_QROWS, NA_KROWS, kr0 - r0 + NA_ROWS - 1)
        bias = jnp.transpose(bias, (0, 3, 1, 4, 2)).reshape((-1,) + flat)
        tabs.append(jnp.where(valid[None], bias, NEG_INF))
    return jnp.stack(tabs, axis=1)


def _na_kernel(q_ref, k_ref, v_ref, tab_ref, init_ref, o_ref, *, tq, tkw, n_lat, n_ctx, nq):
    del init_ref
    i = pl.program_id(2)
    kind = jnp.where(i == 0, 0, jnp.where(i == nq - 1, 2, 1))
    start = pl.multiple_of(jnp.clip(i * tq - (NA_ROWS // 2) * GRID_W, 0, n_lat - tkw), GRID_W)
    q = q_ref[...]
    k_loc = k_ref[pl.ds(start, tkw), :]
    v_loc = v_ref[pl.ds(start, tkw), :]
    k_ctx = k_ref[n_lat:n_lat + n_ctx, :]
    v_ctx = v_ref[n_lat:n_lat + n_ctx, :]
    s_loc = _dot_nt(q, k_loc) + tab_ref[kind]
    s_ctx = _dot_nt(q, k_ctx)
    m = jnp.maximum(jnp.max(s_loc, axis=-1, keepdims=True), jnp.max(s_ctx, axis=-1, keepdims=True))
    p_loc = jnp.exp(s_loc - m)
    p_ctx = jnp.exp(s_ctx - m)
    denom = jnp.sum(p_loc, axis=-1, keepdims=True) + jnp.sum(p_ctx, axis=-1, keepdims=True)
    o = (_dot(p_loc.astype(MXU_DTYPE), v_loc) + _dot(p_ctx.astype(MXU_DTYPE), v_ctx)) / denom
    o_ref[...] = o.astype(o_ref.dtype)


def _neighbourhood_attention(q, k, v, tables, n_lat, n_ctx):
    b, hq, s, d = q.shape
    tq = NA_QROWS * GRID_W
    tkw = NA_KROWS * GRID_W
    nq = n_lat // tq
    return pl.pallas_call(
        functools.partial(_na_kernel, tq=tq, tkw=tkw, n_lat=n_lat, n_ctx=n_ctx, nq=nq),
        grid=(b, hq, nq),
        in_specs=[
            pl.BlockSpec((None, None, tq, d), lambda bi, h, i: (bi, h, i, 0)),
            pl.BlockSpec((None, None, s, d), lambda bi, h, i: (bi, h, 0, 0)),
            pl.BlockSpec((None, None, s, d), lambda bi, h, i: (bi, h, 0, 0)),
            pl.BlockSpec((None, 3, tq, tkw), lambda bi, h, i: (h, 0, 0, 0)),
            pl.BlockSpec(memory_space=pl.ANY),
        ],
        out_specs=pl.BlockSpec((None, None, tq, d), lambda bi, h, i: (bi, h, i, 0)),
        out_shape=jax.ShapeDtypeStruct((b, hq, s, d), MXU_DTYPE),
        input_output_aliases={4: 0},
        compiler_params=_cparams(("arbitrary", "arbitrary", "arbitrary")),
        name="neighbourhood_attention",
    )(q, k, v, tables, jnp.zeros((b, hq, s, d), MXU_DTYPE))


def _outproj_kernel(o1_ref, o2_ref, w_ref, x_ref, mod_ref, rw_ref, rb_ref,
                    xo_ref, h_ref, lg_ref, *, tm, n_lat, n_heads, transposed):
    row0 = pl.program_id(1) * tm
    if transposed:
        yt = jnp.concatenate([o1_ref[...].reshape(-1, tm), o2_ref[...].reshape(-1, tm)], axis=0)
        z = lax.dot_general(yt, w_ref[...], (((0,), (0,)), ((), ())), preferred_element_type=jnp.float32)
    else:
        y = jnp.concatenate([o1_ref[h] for h in range(n_heads)] + [o2_ref[h] for h in range(n_heads)], axis=-1)
        z = _dot(y, w_ref[...])
    rows = row0 + lax.broadcasted_iota(jnp.int32, (tm, 1), 0)
    gate = jnp.where(rows >= n_lat, mod_ref[6:7, :], mod_ref[4:5, :])
    xn = x_ref[...] + gate * z
    xo_ref[...] = xn
    h = _modulated_norm(xn, mod_ref, row0, n_lat, 0, 1, 2, 3)
    h_ref[...] = h
    lg_ref[...] = jnp.dot(h, rw_ref[...], preferred_element_type=jnp.float32,
                          precision=lax.Precision.HIGHEST) + rb_ref[...]


def _outproj(o1, o2, w_out, xs, modv, router_w, router_b, n_lat, transposed):
    b, s, dm = xs.shape
    tm = ROW_TILE
    nh = o1.shape[1]
    nl = router_w.shape[1]
    if transposed:
        o_spec = pl.BlockSpec((None, nh, o1.shape[2], tm), lambda bi, j: (bi, 0, 0, j))
    else:
        o_spec = pl.BlockSpec((None, nh, tm, o1.shape[3]), lambda bi, j: (bi, 0, j, 0))
    return pl.pallas_call(
        functools.partial(_outproj_kernel, tm=tm, n_lat=n_lat, n_heads=nh, transposed=transposed),
        grid=(b, s // tm),
        in_specs=[
            o_spec,
            o_spec,
            pl.BlockSpec(w_out.shape, lambda bi, j: (0, 0)),
            pl.BlockSpec((None, tm, dm), lambda bi, j: (bi, j, 0)),
            pl.BlockSpec((None, 8, dm), lambda bi, j: (bi, 0, 0)),
            pl.BlockSpec(router_w.shape, lambda bi, j: (0, 0)),
            pl.BlockSpec(router_b.shape, lambda bi, j: (0, 0)),
        ],
        out_specs=[pl.BlockSpec((None, tm, dm), lambda bi, j: (bi, j, 0)),
                   pl.BlockSpec((None, tm, dm), lambda bi, j: (bi, j, 0)),
                   pl.BlockSpec((None, tm, nl), lambda bi, j: (bi, j, 0))],
        out_shape=[jax.ShapeDtypeStruct((b, s, dm), jnp.float32),
                   jax.ShapeDtypeStruct((b, s, dm), jnp.float32),
                   jax.ShapeDtypeStruct((b, s, nl), jnp.float32)],
        input_output_aliases={3: 0},
        compiler_params=_cparams(("arbitrary", "arbitrary")),
        name="outproj_residual_norm_router",
    )(o1, o2, w_out, xs, modv, router_w, router_b)


def _expert_kernel(be_ref, tok0_ref, tokn_ref, h_hbm, win_ref, bin_ref, wout_ref, bout_ref,
                   y_ref, xbuf, sems, *, blk, nb):
    del be_ref
    i = pl.program_id(0)
    slot = lax.rem(i, 2)
    nxt = 1 - slot

    def gather_rows(tok_ref, dst_slot, lo, hi):
        for rr in range(lo, hi):
            pltpu.make_async_copy(h_hbm.at[pl.ds(tok_ref[0, rr], 1)], xbuf.at[dst_slot, pl.ds(rr, 1)],
                                  sems.at[dst_slot]).start()

    def wait_slot(s_):
        pltpu.make_async_copy(h_hbm.at[pl.ds(0, blk)], xbuf.at[s_], sems.at[s_]).wait()

    @pl.when(i == 0)
    def _():
        gather_rows(tok0_ref, 0, 0, blk)

    wait_slot(slot)

    n_chunks = 4
    cw = D_EXPERT // n_chunks
    per = blk // (2 * n_chunks)
    x = xbuf[slot].astype(MXU_DTYPE)
    acts = []
    for c in range(n_chunks):
        glu = _dot(x, win_ref[:, c * cw:(c + 1) * cw]) + bin_ref[:, c * cw:(c + 1) * cw]
        gather_rows(tokn_ref, nxt, 2 * c * per, (2 * c + 1) * per)
        lin = (_dot(x, win_ref[:, D_EXPERT + c * cw:D_EXPERT + (c + 1) * cw])
               + bin_ref[:, D_EXPERT + c * cw:D_EXPERT + (c + 1) * cw])
        gather_rows(tokn_ref, nxt, (2 * c + 1) * per, (2 * c + 2) * per)
        glu = jnp.minimum(glu, SWIGLU_LIMIT)
        lin = jnp.clip(lin, -SWIGLU_LIMIT, SWIGLU_LIMIT)
        acts.append((glu * jax.nn.sigmoid(SWIGLU_ALPHA * glu) * (lin + 1.0)).astype(MXU_DTYPE))
    y_ref[...] = _dot(jnp.concatenate(acts, axis=-1), wout_ref[...]) + bout_ref[...]

    @pl.when(i == nb - 1)
    def _():
        wait_slot(nxt)


def _experts(h_flat, blk_expert, tok_buf, w_in, b_in, w_out, b_out):
    n, dm = h_flat.shape
    blk = MOE_ROWS
    nb = blk_expert.shape[0]
    de2 = w_in.shape[-1]
    tok3 = tok_buf.reshape(nb, 1, blk)
    grid_spec = pltpu.PrefetchScalarGridSpec(
        num_scalar_prefetch=1,
        grid=(nb,),
        in_specs=[
            pl.BlockSpec((None, 1, blk), lambda i, be: (0, 0, 0), memory_space=pltpu.SMEM),
            pl.BlockSpec((None, 1, blk), lambda i, be: (jnp.minimum(i + 1, nb - 1), 0, 0),
                         memory_space=pltpu.SMEM),
            pl.BlockSpec(memory_space=pl.ANY),
            pl.BlockSpec((None, dm, de2), lambda i, be: (be[i], 0, 0)),
            pl.BlockSpec((None, 1, de2), lambda i, be: (be[i], 0, 0)),
            pl.BlockSpec((None, de2 // 2, dm), lambda i, be: (be[i], 0, 0)),
            pl.BlockSpec((None, 1, dm), lambda i, be: (be[i], 0, 0)),
        ],
        out_specs=pl.BlockSpec((blk, dm), lambda i, be: (i, 0)),
        scratch_shapes=[pltpu.VMEM((2, blk, dm), jnp.float32), pltpu.SemaphoreType.DMA((2,))],
    )
    return pl.pallas_call(
        functools.partial(_expert_kernel, blk=blk, nb=nb),
        grid_spec=grid_spec,
        out_shape=jax.ShapeDtypeStruct((nb * blk, dm), jnp.float32),
        compiler_params=_cparams(("arbitrary",)),
        name="routed_experts",
    )(blk_expert, tok3, tok3, h_flat, w_in, b_in.reshape(N_EXPERTS, 1, de2), w_out,
      b_out.reshape(N_EXPERTS, 1, dm))


def _combine_kernel(pos0_ref, posn_ref, y_hbm, x_ref, mod_ref, w_ref, o_ref, gbuf, sems, *, tc, nsteps, n_lat,
                    steps_per_batch):
    i = pl.program_id(0)
    slot = lax.rem(i, 2)

    def gather(pos_ref, dst_slot):
        def issue(t, carry):
            for kk in range(TOP_K):
                pltpu.make_async_copy(y_hbm.at[pl.ds(pos_ref[0, t * TOP_K + kk], 1)],
                                      gbuf.at[dst_slot, kk, pl.ds(t, 1)], sems.at[dst_slot]).start()
            return carry
        lax.fori_loop(0, tc, issue, 0, unroll=8)

    @pl.when(i == 0)
    def _():
        gather(pos0_ref, 0)

    @pl.when(i + 1 < nsteps)
    def _():
        gather(posn_ref, 1 - slot)

    for kk in range(TOP_K):
        pltpu.make_async_copy(y_hbm.at[pl.ds(0, tc)], gbuf.at[slot, kk], sems.at[slot]).wait()

    w = w_ref[...]
    y = gbuf[slot, 0] * w[:, 0:1]
    for kk in range(1, TOP_K):
        y = y + gbuf[slot, kk] * w[:, kk:kk + 1]
    row0 = lax.rem(i, steps_per_batch) * tc
    rows = row0 + lax.broadcasted_iota(jnp.int32, (tc, 1), 0)
    gate = jnp.where(rows >= n_lat, mod_ref[7:8, :], mod_ref[5:6, :])
    o_ref[...] = x_ref[...] + gate * y


def _combine(y_sorted, pos, weights, xs_flat, modv, s, n_lat):
    n, dm = xs_flat.shape
    tc = COMBINE_ROWS
    nsteps = n // tc
    spb = s // tc
    pos3 = pos.reshape(nsteps, 1, tc * TOP_K)
    return pl.pallas_call(
        functools.partial(_combine_kernel, tc=tc, nsteps=nsteps, n_lat=n_lat, steps_per_batch=spb),
        grid=(nsteps,),
        in_specs=[
            pl.BlockSpec((None, 1, tc * TOP_K), lambda i: (0, 0, 0), memory_space=pltpu.SMEM),
            pl.BlockSpec((None, 1, tc * TOP_K), lambda i: (jnp.minimum(i + 1, nsteps - 1), 0, 0),
                         memory_space=pltpu.SMEM),
            pl.BlockSpec(memory_space=pl.ANY),
            pl.BlockSpec((tc, dm), lambda i: (i, 0)),
            pl.BlockSpec((None, 8, dm), lambda i: (i // spb, 0, 0)),
            pl.BlockSpec((tc, TOP_K), lambda i: (i, 0)),
        ],
        out_specs=pl.BlockSpec((tc, dm), lambda i: (i, 0)),
        out_shape=jax.ShapeDtypeStruct((n, dm), jnp.float32),
        scratch_shapes=[pltpu.VMEM((2, TOP_K, tc, dm), jnp.float32), pltpu.SemaphoreType.DMA((2,))],
        input_output_aliases={3: 0},
        compiler_params=_cparams(("arbitrary",)),
        name="expert_combine",
    )(pos3, pos3, y_sorted, xs_flat, modv, weights)


def _final_norm_kernel(x_ref, g_ref, o_ref):
    x = x_ref[...]
    o_ref[...] = x * lax.rsqrt(jnp.mean(x * x, axis=-1, keepdims=True) + NORM_EPS) * g_ref[...]


def _final_norm(xs, g, n_lat):
    b, s, dm = xs.shape
    tm = 512
    return pl.pallas_call(
        _final_norm_kernel,
        grid=(b, n_lat // tm),
        in_specs=[pl.BlockSpec((None, tm, dm), lambda bi, j: (bi, j, 0)),
                  pl.BlockSpec((1, dm), lambda bi, j: (0, 0))],
        out_specs=pl.BlockSpec((None, tm, dm), lambda bi, j: (bi, j, 0)),
        out_shape=jax.ShapeDtypeStruct((b, n_lat, dm), jnp.float32),
        compiler_params=_cparams(("arbitrary", "arbitrary")),
        name="final_norm",
    )(xs, g.reshape(1, dm))


def _swap_perm(width, dim):
    j = np.arange(width)
    half = dim // 2
    return np.where((j % dim) < half, j + half, j - half)


def _rope_tables(n_lat, n_ctx, dim):
    t = jnp.arange(n_lat, dtype=jnp.int32)
    row = (t // GRID_W).astype(jnp.float32)
    col = (t % GRID_W).astype(jnp.float32)
    quarter = dim // 4
    inv_freq = ROPE_BASE ** (-jnp.arange(quarter, dtype=jnp.float32) / quarter)
    ang = jnp.concatenate([row[:, None] * inv_freq, col[:, None] * inv_freq], axis=-1)
    cos, sin = jnp.cos(ang), jnp.sin(ang)
    reps = LANES // dim
    cos_t = jnp.tile(jnp.concatenate([cos, cos], axis=-1), (1, reps))
    sin_t = jnp.tile(jnp.concatenate([-sin, sin], axis=-1), (1, reps))
    cos_t = jnp.concatenate([cos_t, jnp.ones((n_ctx, LANES), jnp.float32)], axis=0)
    sin_t = jnp.concatenate([sin_t, jnp.zeros((n_ctx, LANES), jnp.float32)], axis=0)
    return cos_t, sin_t


def _mod_rows(mod_l, norm_g, b, first):
    lat, ctx = mod_l[:b], mod_l[b]
    a_lat = norm_g[None] * (1.0 + lat[:, first + 1])
    b_lat = lat[:, first]
    a_ctx = jnp.broadcast_to(norm_g * (1.0 + ctx[first + 1]), a_lat.shape)
    b_ctx = jnp.broadcast_to(ctx[first], a_lat.shape)
    g_mix_ctx = jnp.broadcast_to(ctx[2], a_lat.shape)
    g_ffn_ctx = jnp.broadcast_to(ctx[5], a_lat.shape)
    return jnp.stack([a_lat, b_lat, a_ctx, b_ctx, lat[:, 2], lat[:, 5], g_mix_ctx, g_ffn_ctx], axis=1)


def _route(logits, blk):
    n = logits.shape[0]
    nk = n * TOP_K
    top_val, top_idx = lax.top_k(logits, TOP_K)
    gate = jax.nn.softmax(top_val, axis=-1)
    expert = top_idx.reshape(-1)
    onehot = (expert[:, None] == jnp.arange(N_EXPERTS, dtype=jnp.int32)[None, :]).astype(jnp.int32)
    csum = jnp.cumsum(onehot, axis=0)
    rank = jnp.take_along_axis(csum, expert[:, None], axis=1)[:, 0] - 1
    sizes = csum[-1]
    padded = (sizes + blk - 1) // blk * blk
    pends = jnp.cumsum(padded)
    pstarts = pends - padded
    dest = pstarts[expert] + rank
    nb = -(-(nk + N_EXPERTS * (blk - 1)) // blk)
    blk_expert = jnp.minimum(jnp.searchsorted(pends, jnp.arange(nb, dtype=jnp.int32) * blk, side='right'),
                             N_EXPERTS - 1).astype(jnp.int32)
    order = jnp.argsort(expert)
    starts = jnp.cumsum(sizes) - sizes
    slot = jnp.arange(nb * blk, dtype=jnp.int32)
    slot_expert = jnp.repeat(blk_expert, blk)
    slot_rank = slot - pstarts[slot_expert]
    src = jnp.minimum(starts[slot_expert] + slot_rank, nk - 1)
    tok_buf = jnp.where(slot_rank < sizes[slot_expert], order[src] // TOP_K, 0).astype(jnp.int32)
    return blk_expert, tok_buf, gate, dest.astype(jnp.int32)


def kernel(x, c, ctx, c_ctx, mod_w, mod_b, norm_mix, norm_ffn, ab_w_in, ab_w_out, a_sink, b_rpb,
           cd_w_in, c_q_norm, c_w_q_b, c_kv_norm, c_w_kv_b, d_q_norm, d_k_norm, cd_w_out,
           router_w, router_b, exp_w_in, exp_b_in, exp_w_out, exp_b_out, final_norm):
    b, t, dm = x.shape
    n_ctx = ctx.shape[1]
    s = t + n_ctx
    depth = mod_w.shape[0]
    rows = t // GRID_W
    d = HEAD_DIM
    f32 = jnp.float32

    xs = jnp.concatenate([x, ctx], axis=1)

    cond = jnp.zeros((8, dm), f32).at[:b].set(c).at[b].set(c_ctx)
    mod_all = _modulation(cond, mod_w, mod_b)[:, :b + 1].reshape(depth, b + 1, 6, dm)

    cos_t, sin_t = _rope_tables(t, n_ctx, d)
    cosm_t, sinm_t = _rope_tables(t, n_ctx, C_ROPE)

    router_w_p = jnp.zeros((depth, dm, LANES), f32).at[:, :, :N_EXPERTS].set(router_w)
    router_b_p = jnp.zeros((depth, 1, LANES), f32).at[:, 0, :N_EXPERTS].set(router_b)

    p64_512 = _swap_perm(512, d)
    p64_128 = _swap_perm(128, d)
    p32 = _swap_perm(C_ROPE, C_ROPE)

    for layer in range(depth):
        i = layer // 2
        modv_mix = _mod_rows(mod_all[layer], norm_mix[layer], b, 0)
        modv_ffn = _mod_rows(mod_all[layer], norm_ffn[layer], b, 3)
        if layer % 2 == 0:
            w = ab_w_in[i]
            w_ext = jnp.concatenate([w, w[:, 0:512][:, p64_512], w[:, 512:640][:, p64_128]], axis=1).astype(MXU_DTYPE)
            qa, ka, va, qb, kb, vb = _proj_ab(xs, modv_mix, w_ext, cos_t, sin_t, t)
            sink_gr = a_sink[i].astype(f32).reshape(A_KV_HEADS, A_HEADS // A_KV_HEADS)
            r = A_HEADS // A_KV_HEADS
            sink_win = jnp.repeat(sink_gr, WIN_Q, axis=1).reshape(A_KV_HEADS, r * WIN_Q, 1)
            sink_ctx = jnp.repeat(sink_gr, n_ctx, axis=1).reshape(A_KV_HEADS, r * n_ctx, 1)
            o1 = _window_attention(qa, ka, va, sink_win, t, n_ctx)
            o1 = _flash(qa, ka, va, groups=A_KV_HEADS, tq=n_ctx, tk=n_ctx, q_row0=t, n_q=n_ctx,
                        kv_row0=t, n_kv=n_ctx, sink=sink_ctx, out=o1)
            tables = _na_tables(b_rpb[i], rows)
            o2 = _neighbourhood_attention(qb, kb, vb, tables, t, n_ctx)
            o2 = _flash(qb, kb, vb, groups=B_HEADS, tq=n_ctx, tk=n_ctx, q_row0=t, n_q=n_ctx,
                        kv_row0=t, n_kv=n_ctx, out=o2)
            w_out = ab_w_out[i].astype(MXU_DTYPE)
        else:
            w = cd_w_in[i]
            base = C_Q_RANK + C_KV_RANK
            o_qd, o_kd, o_vd = base + C_ROPE, base + C_ROPE + 512, base + C_ROPE + 640
            wqd, wkd = w[:, o_qd:o_qd + 512], w[:, o_kd:o_kd + 128]
            gq = jnp.tile(d_q_norm[i].astype(f32), D_HEADS)
            gk = jnp.tile(d_k_norm[i].astype(f32), D_KV_HEADS)
            krope = w[:, base:base + C_ROPE]
            pad = jnp.zeros((dm, LANES - C_ROPE), f32)
            w1 = jnp.concatenate([
                w[:, 0:base], wqd, wkd, w[:, o_vd:o_vd + 128],
                (wqd * gq[None])[:, p64_512], (wkd * gk[None])[:, p64_128],
                krope, pad, krope[:, p32], pad], axis=1).astype(MXU_DTYPE)
            dq = C_NOPE + C_ROPE
            wq = c_w_q_b[i].reshape(C_Q_RANK, C_HEADS, dq)
            wq_nope = wq[:, :, :C_NOPE].reshape(C_Q_RANK, C_HEADS * C_NOPE)
            wq_rope = wq[:, :, C_NOPE:].reshape(C_Q_RANK, C_HEADS * C_ROPE)
            wq2 = jnp.concatenate([wq_nope, wq_rope, wq_rope[:, _swap_perm(C_HEADS * C_ROPE, C_ROPE)]],
                                  axis=1).astype(MXU_DTYPE)
            wkv = c_w_kv_b[i].reshape(C_KV_RANK, C_HEADS, C_NOPE + C_V)
            wkv2 = jnp.concatenate([wkv[:, :, :C_NOPE].reshape(C_KV_RANK, -1),
                                    wkv[:, :, C_NOPE:].reshape(C_KV_RANK, -1)], axis=1).astype(MXU_DTYPE)
            qc, kc, vc, qd, kd, vd = _proj_cd(
                xs, modv_mix, w1, wq2, wkv2, c_q_norm[i].astype(f32).reshape(1, -1),
                c_kv_norm[i].astype(f32).reshape(1, -1), gq.reshape(1, -1), gk.reshape(1, -1),
                cos_t, sin_t, cosm_t, sinm_t, t)
            o1 = _flash_t(qc, kc, vc, groups=C_HEADS, tq=FLASH_M, tk=FLASH_TK, q_row0=0, n_q=t, kv_row0=0, n_kv=s,
                          out=jnp.zeros((b, C_HEADS, C_V, s), MXU_DTYPE))
            o1 = _flash_t(qc, kc, vc, groups=C_HEADS, tq=n_ctx, tk=n_ctx, q_row0=t, n_q=n_ctx,
                          kv_row0=t, n_kv=n_ctx, out=o1)
            rd = D_HEADS // D_KV_HEADS
            o2 = _flash_t(qd, kd, vd, groups=D_KV_HEADS, tq=FLASH_M // rd, tk=FLASH_TK, q_row0=0, n_q=t,
                          kv_row0=0, n_kv=s, out=jnp.zeros((b, D_HEADS, d, s), MXU_DTYPE))
            o2 = _flash_t(qd, kd, vd, groups=D_KV_HEADS, tq=n_ctx, tk=n_ctx, q_row0=t, n_q=n_ctx,
                          kv_row0=t, n_kv=n_ctx, out=o2)
            w_out = cd_w_out[i].astype(MXU_DTYPE)

        xs, h2, logits = _outproj(o1, o2, w_out, xs, modv_ffn, router_w_p[layer], router_b_p[layer], t,
                                  transposed=layer % 2 == 1)

        n = b * s
        blk_expert, tok_buf, weights, pos = _route(logits.reshape(n, LANES)[:, :N_EXPERTS], MOE_ROWS)
        y_sorted = _experts(h2.reshape(n, dm), blk_expert, tok_buf,
                            exp_w_in[layer].astype(MXU_DTYPE), exp_b_in[layer].astype(f32),
                            exp_w_out[layer].astype(MXU_DTYPE), exp_b_out[layer].astype(f32))
        xs = _combine(y_sorted, pos, weights, xs.reshape(n, dm), modv_ffn, s, t).reshape(b, s, dm)

    return _final_norm(xs, final_norm.astype(f32), t)
```

```python
import functools

import jax
import jax.numpy as jnp
import numpy as np
from jax import lax
from jax.experimental import pallas as pl
from jax.experimental.pallas import tpu as pltpu

GRID_W = 64
HEAD_DIM = 64
ROPE_BASE = 10000.0
NORM_EPS = 1e-6
NEG_INF = -1e30
A_HEADS = 8
A_KV_HEADS = 2
A_WINDOW = 128
B_HEADS = 8
NA_ROWS = 8
NA_COLS = 16
C_HEADS = 8
C_Q_RANK = 768
C_KV_RANK = 256
C_NOPE = 64
C_ROPE = 32
C_V = 64
D_HEADS = 8
D_KV_HEADS = 2
N_EXPERTS = 32
TOP_K = 4
D_EXPERT = 1024
SWIGLU_LIMIT = 7.0
SWIGLU_ALPHA = 1.702
LOG2_E = 1.4426950408889634

LANES = 128
ROW_TILES = 8
VMEM_LIMIT_BYTES = 56 * 1024 * 1024

MXU_DTYPE = jnp.bfloat16

ROW_TILE = 640
MOE_ROWS = 256
COMBINE_ROWS = 256
NA_QROWS = 8
NA_KROWS = 16
WIN_Q = 256
FLASH_M = 1024
FLASH_TK = 640


def _cparams(sem, vmem=VMEM_LIMIT_BYTES):
    return pltpu.CompilerParams(dimension_semantics=sem, vmem_limit_bytes=vmem)


def _dot(a, b):
    return jnp.dot(a, b, preferred_element_type=jnp.float32)


def _dot_nt(a, b):
    return lax.dot_general(a, b, (((1,), (1,)), ((), ())), preferred_element_type=jnp.float32)


def _mod_kernel(c_ref, w_ref, b_ref, o_ref):
    c = c_ref[...]
    s = c * jax.nn.sigmoid(c)
    o_ref[...] = jnp.dot(s, w_ref[...], preferred_element_type=jnp.float32,
                         precision=lax.Precision.HIGHEST) + b_ref[...]


def _modulation(cond, mod_w, mod_b):
    depth, dm, n = mod_w.shape
    tn = 1536
    return pl.pallas_call(
        _mod_kernel,
        grid=(depth, n // tn),
        in_specs=[
            pl.BlockSpec((8, dm), lambda l, j: (0, 0)),
            pl.BlockSpec((None, dm, tn), lambda l, j: (l, 0, j)),
            pl.BlockSpec((None, 1, tn), lambda l, j: (l, 0, j)),
        ],
        out_specs=pl.BlockSpec((None, 8, tn), lambda l, j: (l, 0, j)),
        out_shape=jax.ShapeDtypeStruct((depth, 8, n), jnp.float32),
        compiler_params=_cparams(("arbitrary", "arbitrary")),
        name="adaln_modulation",
    )(cond, mod_w, mod_b.reshape(depth, 1, n))


def _modulated_norm(x, mod_ref, row0, n_lat, a_lat, b_lat, a_ctx, b_ctx):
    tm = x.shape[0]
    ms = jnp.mean(x * x, axis=-1, keepdims=True)
    xh = x * lax.rsqrt(ms + NORM_EPS)
    rows = row0 + lax.broadcasted_iota(jnp.int32, (tm, 1), 0)
    is_ctx = rows >= n_lat
    a = jnp.where(is_ctx, mod_ref[a_ctx:a_ctx + 1, :], mod_ref[a_lat:a_lat + 1, :])
    b = jnp.where(is_ctx, mod_ref[b_ctx:b_ctx + 1, :], mod_ref[b_lat:b_lat + 1, :])
    return xh * a + b


def _store_heads(o_ref, val, n_heads, width, lane0=0):
    for h in range(n_heads):
        o_ref[h, :, lane0:lane0 + width] = val[:, h * width:(h + 1) * width].astype(o_ref.dtype)


def _store_row_tiles(o_ref, val):
    rows = val.shape[0]
    for c in range(ROW_TILES):
        o_ref[pl.ds(c, rows, stride=ROW_TILES), :] = val[:, c * LANES:(c + 1) * LANES]


def _load_row_tiles(ref, rows):
    return jnp.concatenate([ref[pl.ds(c, rows, stride=ROW_TILES), :] for c in range(ROW_TILES)], axis=-1)


def _tile_lanes(t, reps):
    return t if reps == 1 else jnp.concatenate([t] * reps, axis=-1)


def _proj_ab_kernel(x_ref, mod_ref, w_ref, cos_ref, sin_ref,
                    qa_ref, ka_ref, va_ref, qb_ref, kb_ref, vb_ref, *, tm, n_lat):
    row0 = pl.program_id(1) * tm
    h = _modulated_norm(x_ref[...], mod_ref, row0, n_lat, 0, 1, 2, 3).astype(MXU_DTYPE)
    cos = cos_ref[...]
    sin = sin_ref[...]
    d = HEAD_DIM
    scale = d ** -0.5
    qa = _dot(h, w_ref[:, 0:512]) * _tile_lanes(cos, 4) + _dot(h, w_ref[:, 2304:2816]) * _tile_lanes(sin, 4)
    _store_heads(qa_ref, qa * scale, A_HEADS, d)
    ka = _dot(h, w_ref[:, 512:640]) * cos + _dot(h, w_ref[:, 2816:2944]) * sin
    _store_heads(ka_ref, ka, A_KV_HEADS, d)
    _store_heads(va_ref, _dot(h, w_ref[:, 640:768]), A_KV_HEADS, d)
    _store_heads(qb_ref, _dot(h, w_ref[:, 768:1280]) * scale, B_HEADS, d)
    _store_heads(kb_ref, _dot(h, w_ref[:, 1280:1792]), B_HEADS, d)
    _store_heads(vb_ref, _dot(h, w_ref[:, 1792:2304]), B_HEADS, d)


def _proj_ab(xs, modv, w_ext, cos_t, sin_t, n_lat):
    b, s, dm = xs.shape
    tm = ROW_TILE
    d = HEAD_DIM
    nw = w_ext.shape[1]

    def head_spec(nh):
        return pl.BlockSpec((None, nh, tm, d), lambda bi, j: (bi, 0, j, 0))

    def head_shape(nh):
        return jax.ShapeDtypeStruct((b, nh, s, d), MXU_DTYPE)

    return pl.pallas_call(
        functools.partial(_proj_ab_kernel, tm=tm, n_lat=n_lat),
        grid=(b, s // tm),
        in_specs=[
            pl.BlockSpec((None, tm, dm), lambda bi, j: (bi, j, 0)),
            pl.BlockSpec((None, 8, dm), lambda bi, j: (bi, 0, 0)),
            pl.BlockSpec((dm, nw), lambda bi, j: (0, 0)),
            pl.BlockSpec((tm, LANES), lambda bi, j: (j, 0)),
            pl.BlockSpec((tm, LANES), lambda bi, j: (j, 0)),
        ],
        out_specs=[head_spec(A_HEADS), head_spec(A_KV_HEADS), head_spec(A_KV_HEADS),
                   head_spec(B_HEADS), head_spec(B_HEADS), head_spec(B_HEADS)],
        out_shape=[head_shape(A_HEADS), head_shape(A_KV_HEADS), head_shape(A_KV_HEADS),
                   head_shape(B_HEADS), head_shape(B_HEADS), head_shape(B_HEADS)],
        compiler_params=_cparams(("arbitrary", "arbitrary")),
        name="proj_ab",
    )(xs, modv, w_ext, cos_t, sin_t)


def _proj_cd_kernel(x_ref, mod_ref, w1_ref, wq2_ref, wkv2_ref, qn_ref, kvn_ref, gq_ref, gk_ref,
                    cos_ref, sin_ref, cosm_ref, sinm_ref,
                    qc_ref, kc_ref, vc_ref, qd_ref, kd_ref, vd_ref, *, tm, n_lat):
    row0 = pl.program_id(1) * tm
    h = _modulated_norm(x_ref[...], mod_ref, row0, n_lat, 0, 1, 2, 3).astype(MXU_DTYPE)
    cos = cos_ref[...]
    sin = sin_ref[...]
    cosm = cosm_ref[...]
    sinm = sinm_ref[...]
    d = HEAD_DIM

    cq = _dot(h, w1_ref[:, 0:768])
    cqn = cq * lax.rsqrt(jnp.mean(cq * cq, axis=-1, keepdims=True) + NORM_EPS) * qn_ref[...]
    q2 = _dot(cqn.astype(MXU_DTYPE), wq2_ref[...])
    c_scale = (C_NOPE + C_ROPE) ** -0.5 * LOG2_E
    q_rope = (q2[:, 512:768] * _tile_lanes(cosm, 2) + q2[:, 768:1024] * _tile_lanes(sinm, 2)) * c_scale
    _store_heads(qc_ref, q2[:, 0:512] * c_scale, C_HEADS, C_NOPE)
    _store_heads(qc_ref, q_rope, C_HEADS, C_ROPE, lane0=C_NOPE)

    ckv = _dot(h, w1_ref[:, 768:1024])
    ckvn = ckv * lax.rsqrt(jnp.mean(ckv * ckv, axis=-1, keepdims=True) + NORM_EPS) * kvn_ref[...]
    kv2 = _dot(ckvn.astype(MXU_DTYPE), wkv2_ref[...])
    k_rope = (_dot(h, w1_ref[:, 2432:2560]) * cosm + _dot(h, w1_ref[:, 2560:2688]) * sinm)[:, 0:C_ROPE]
    _store_heads(kc_ref, kv2[:, 0:512], C_HEADS, C_NOPE)
    for hh in range(C_HEADS):
        kc_ref[hh, :, C_NOPE:C_NOPE + C_ROPE] = k_rope.astype(kc_ref.dtype)
    _store_heads(vc_ref, kv2[:, 512:1024], C_HEADS, C_V)

    def normed_rope(raw, swapped_gained, gain, n_heads, o_ref, scale):
        reps = n_heads * d // LANES
        val = raw * (_tile_lanes(cos, reps) * gain) + swapped_gained * _tile_lanes(sin, reps)
        for hh in range(n_heads):
            sl = raw[:, hh * d:(hh + 1) * d]
            r = lax.rsqrt(jnp.mean(sl * sl, axis=-1, keepdims=True) + NORM_EPS)
            o_ref[hh, :, :] = (val[:, hh * d:(hh + 1) * d] * (r * scale)).astype(o_ref.dtype)

    normed_rope(_dot(h, w1_ref[:, 1024:1536]), _dot(h, w1_ref[:, 1792:2304]), gq_ref[...], D_HEADS, qd_ref,
                d ** -0.5 * LOG2_E)
    normed_rope(_dot(h, w1_ref[:, 1536:1664]), _dot(h, w1_ref[:, 2304:2432]), gk_ref[...], D_KV_HEADS, kd_ref, 1.0)
    _store_heads(vd_ref, _dot(h, w1_ref[:, 1664:1792]), D_KV_HEADS, d)


def _proj_cd(xs, modv, w1, wq2, wkv2, qn, kvn, gq, gk, cos_t, sin_t, cosm_t, sinm_t, n_lat):
    b, s, dm = xs.shape
    tm = ROW_TILE
    dk_c = C_NOPE + C_ROPE

    def full(a):
        return pl.BlockSpec(a.shape, lambda bi, j: (0,) * a.ndim)

    def head_spec(nh, w):
        return pl.BlockSpec((None, nh, tm, w), lambda bi, j: (bi, 0, j, 0))

    def head_shape(nh, w):
        return jax.ShapeDtypeStruct((b, nh, s, w), MXU_DTYPE)

    tab = pl.BlockSpec((tm, LANES), lambda bi, j: (j, 0))
    return pl.pallas_call(
        functools.partial(_proj_cd_kernel, tm=tm, n_lat=n_lat),
        grid=(b, s // tm),
        in_specs=[
            pl.BlockSpec((None, tm, dm), lambda bi, j: (bi, j, 0)),
            pl.BlockSpec((None, 8, dm), lambda bi, j: (bi, 0, 0)),
            full(w1), full(wq2), full(wkv2), full(qn), full(kvn), full(gq), full(gk),
            tab, tab, tab, tab,
        ],
        out_specs=[head_spec(C_HEADS, dk_c), head_spec(C_HEADS, dk_c), head_spec(C_HEADS, C_V),
                   head_spec(D_HEADS, HEAD_DIM), head_spec(D_KV_HEADS, HEAD_DIM), head_spec(D_KV_HEADS, HEAD_DIM)],
        out_shape=[head_shape(C_HEADS, dk_c), head_shape(C_HEADS, dk_c), head_shape(C_HEADS, C_V),
                   head_shape(D_HEADS, HEAD_DIM), head_shape(D_KV_HEADS, HEAD_DIM), head_shape(D_KV_HEADS, HEAD_DIM)],
        compiler_params=_cparams(("arbitrary", "arbitrary")),
        name="proj_cd",
    )(xs, modv, w1, wq2, wkv2, qn, kvn, gq, gk, cos_t, sin_t, cosm_t, sinm_t)


def _flash_kernel(*refs, r, tq, tk, nk, has_sink):
    if has_sink:
        q_ref, k_ref, v_ref, sink_ref, o_ref, m_sc, l_sc, acc_sc = refs
    else:
        q_ref, k_ref, v_ref, o_ref, m_sc, l_sc, acc_sc = refs
    m_rows = r * tq
    q = q_ref[...].reshape(m_rows, q_ref.shape[-1])
    if has_sink:
        m_sc[...] = sink_ref[...]
        l_sc[...] = jnp.ones_like(l_sc)
    else:
        m_sc[...] = jnp.full_like(m_sc, NEG_INF)
        l_sc[...] = jnp.zeros_like(l_sc)
    acc_sc[...] = jnp.zeros_like(acc_sc)

    def body(j, carry):
        k0 = pl.multiple_of(j * tk, tk)
        k = k_ref[pl.ds(k0, tk), :]
        v = v_ref[pl.ds(k0, tk), :]
        s = _dot_nt(q, k)
        m_prev = m_sc[...]
        m_new = jnp.maximum(m_prev, jnp.max(s, axis=-1, keepdims=True))
        alpha = jnp.exp(m_prev - m_new)
        p = jnp.exp(s - m_new)
        l_sc[...] = alpha * l_sc[...] + jnp.sum(p, axis=-1, keepdims=True)
        acc_sc[...] = alpha * acc_sc[...] + _dot(p.astype(MXU_DTYPE), v)
        m_sc[...] = m_new
        return carry

    lax.fori_loop(0, nk, body, 0)
    o = acc_sc[...] / l_sc[...]
    o_ref[...] = o.reshape(o_ref.shape).astype(o_ref.dtype)


def _flash(q, k, v, *, groups, tq, tk, q_row0, n_q, kv_row0, n_kv, sink=None, out=None):
    b, hq, s, dk = q.shape
    dv = v.shape[-1]
    r = hq // groups
    q5 = q.reshape(b, groups, r, s, dk)
    nk = n_kv // tk
    has_sink = sink is not None
    m_rows = r * tq
    in_specs = [
        pl.BlockSpec((None, None, r, tq, dk), lambda bi, g, i: (bi, g, 0, q_row0 // tq + i, 0)),
        pl.BlockSpec((None, None, n_kv, dk), lambda bi, g, i: (bi, g, kv_row0 // n_kv, 0)),
        pl.BlockSpec((None, None, n_kv, dv), lambda bi, g, i: (bi, g, kv_row0 // n_kv, 0)),
    ]
    args = [q5, k, v]
    if has_sink:
        in_specs.append(pl.BlockSpec((None, m_rows, 1), lambda bi, g, i: (g, 0, 0)))
        args.append(sink)
    aliases = {}
    if out is not None:
        in_specs.append(pl.BlockSpec(memory_space=pl.ANY))
        args.append(out.reshape(b, groups, r, s, dv))
        aliases = {len(args) - 1: 0}
    kern = functools.partial(_flash_kernel, r=r, tq=tq, tk=tk, nk=nk, has_sink=has_sink)
    if out is not None:
        inner = kern
        kern = lambda *refs: inner(*refs[:len(args) - 1], *refs[len(args):])
    o = pl.pallas_call(
        kern,
        grid=(b, groups, n_q // tq),
        in_specs=in_specs,
        out_specs=pl.BlockSpec((None, None, r, tq, dv), lambda bi, g, i: (bi, g, 0, q_row0 // tq + i, 0)),
        out_shape=jax.ShapeDtypeStruct((b, groups, r, s, dv), MXU_DTYPE),
        scratch_shapes=[pltpu.VMEM((m_rows, 1), jnp.float32), pltpu.VMEM((m_rows, 1), jnp.float32),
                        pltpu.VMEM((m_rows, dv), jnp.float32)],
        input_output_aliases=aliases,
        compiler_params=_cparams(("arbitrary", "arbitrary", "arbitrary")),
        name="dense_attention",
    )(*args)
    return o.reshape(b, hq, s, dv)


def _flash_t_kernel(q_ref, k_ref, vt_ref, init_ref, o_ref, m_sc, l_sc, acc_sc, sa_sc, sb_sc, *, r, tq, tk, nk):
    del init_ref
    m_cols = r * tq
    q = q_ref[...].reshape(m_cols, q_ref.shape[-1])
    m_sc[...] = jnp.full_like(m_sc, NEG_INF)
    l_sc[...] = jnp.zeros_like(l_sc)
    acc_sc[...] = jnp.zeros_like(acc_sc)

    def scores(j, s_ref):
        k0 = pl.multiple_of(j * tk, tk)
        s_ref[...] = _dot_nt(k_ref[pl.ds(k0, tk), :], q)

    def update(j, s_ref):
        st = s_ref[...]
        m_prev = m_sc[...]
        m_new = jnp.maximum(m_prev, jnp.max(st, axis=0, keepdims=True))
        alpha = jnp.exp2(m_prev - m_new)
        pt = jnp.exp2(st - m_new)
        l_sc[...] = alpha * l_sc[...] + jnp.sum(pt, axis=0, keepdims=True)
        acc_sc[...] = alpha * acc_sc[...] + _dot(vt_ref[j], pt.astype(MXU_DTYPE))
        m_sc[...] = m_new

    scores(0, sa_sc)

    def body(jj, carry):
        j = 2 * jj
        scores(j + 1, sb_sc)
        update(j, sa_sc)
        scores(j + 2, sa_sc)
        update(j + 1, sb_sc)
        return carry

    lax.fori_loop(0, (nk - 1) // 2, body, 0)
    if nk % 2 == 0:
        scores(nk - 1, sb_sc)
        update(nk - 2, sa_sc)
        update(nk - 1, sb_sc)
    else:
        update(nk - 1, sa_sc)
    o = acc_sc[...] / l_sc[...]
    for rr in range(r):
        o_ref[rr] = o[:, rr * tq:(rr + 1) * tq].astype(o_ref.dtype)


def _flash_t(q, k, v, *, groups, tq, tk, q_row0, n_q, kv_row0, n_kv, out):
    b, hq, s, dk = q.shape
    dv = v.shape[-1]
    r = hq // groups
    q5 = q.reshape(b, groups, r, s, dk)
    nk = n_kv // tk
    vt = jnp.swapaxes(lax.slice_in_dim(v, kv_row0, kv_row0 + n_kv, axis=2).reshape(b, groups, nk, tk, dv), 3, 4)
    m_cols = r * tq
    o = pl.pallas_call(
        functools.partial(_flash_t_kernel, r=r, tq=tq, tk=tk, nk=nk),
        grid=(b, groups, n_q // tq),
        in_specs=[
            pl.BlockSpec((None, None, r, tq, dk), lambda bi, g, i: (bi, g, 0, q_row0 // tq + i, 0)),
            pl.BlockSpec((None, None, n_kv, dk), lambda bi, g, i: (bi, g, kv_row0 // n_kv, 0)),
            pl.BlockSpec((None, None, nk, dv, tk), lambda bi, g, i: (bi, g, 0, 0, 0)),
            pl.BlockSpec(memory_space=pl.ANY),
        ],
        out_specs=pl.BlockSpec((None, None, r, dv, tq), lambda bi, g, i: (bi, g, 0, 0, q_row0 // tq + i)),
        out_shape=jax.ShapeDtypeStruct((b, groups, r, dv, s), MXU_DTYPE),
        scratch_shapes=[pltpu.VMEM((1, m_cols), jnp.float32), pltpu.VMEM((1, m_cols), jnp.float32),
                        pltpu.VMEM((dv, m_cols), jnp.float32),
                        pltpu.VMEM((tk, m_cols), jnp.float32), pltpu.VMEM((tk, m_cols), jnp.float32)],
        input_output_aliases={3: 0},
        compiler_params=_cparams(("arbitrary", "arbitrary", "arbitrary")),
        name="dense_attention_t",
    )(q5, k, vt, out.reshape(b, groups, r, dv, s))
    return o.reshape(b, hq, dv, s)


def _window_kernel(q_ref, k_ref, v_ref, sink_ref, init_ref, o_ref, *, r, tq, n_lat, n_ctx):
    del init_ref
    i = pl.program_id(2)
    span = tq + 2 * A_WINDOW
    d = q_ref.shape[-1]
    q = q_ref[...].reshape(r * tq, d)
    start = pl.multiple_of(jnp.clip(i * tq - A_WINDOW, 0, n_lat - span), A_WINDOW)
    k_loc = k_ref[pl.ds(start, span), :]
    v_loc = v_ref[pl.ds(start, span), :]
    k_ctx = k_ref[n_lat:n_lat + n_ctx, :]
    v_ctx = v_ref[n_lat:n_lat + n_ctx, :]
    qpos = i * tq + lax.rem(lax.broadcasted_iota(jnp.int32, (r * tq, span), 0), tq)
    kpos = start + lax.broadcasted_iota(jnp.int32, (r * tq, span), 1)
    ok = jnp.abs(qpos - kpos) <= A_WINDOW
    s_loc = jnp.where(ok, _dot_nt(q, k_loc), NEG_INF)
    s_ctx = _dot_nt(q, k_ctx)
    sink = sink_ref[...]
    m = jnp.maximum(jnp.maximum(jnp.max(s_loc, axis=-1, keepdims=True), jnp.max(s_ctx, axis=-1, keepdims=True)), sink)
    p_loc = jnp.exp(s_loc - m)
    p_ctx = jnp.exp(s_ctx - m)
    denom = jnp.sum(p_loc, axis=-1, keepdims=True) + jnp.sum(p_ctx, axis=-1, keepdims=True) + jnp.exp(sink - m)
    o = (_dot(p_loc.astype(MXU_DTYPE), v_loc) + _dot(p_ctx.astype(MXU_DTYPE), v_ctx)) / denom
    o_ref[...] = o.reshape(o_ref.shape).astype(o_ref.dtype)


def _window_attention(q, k, v, sink_col, n_lat, n_ctx):
    b, hq, s, d = q.shape
    g = k.shape[1]
    r = hq // g
    tq = WIN_Q
    q5 = q.reshape(b, g, r, s, d)
    o = pl.pallas_call(
        functools.partial(_window_kernel, r=r, tq=tq, n_lat=n_lat, n_ctx=n_ctx),
        grid=(b, g, n_lat // tq),
        in_specs=[
            pl.BlockSpec((None, None, r, tq, d), lambda bi, gi, i: (bi, gi, 0, i, 0)),
            pl.BlockSpec((None, None, s, d), lambda bi, gi, i: (bi, gi, 0, 0)),
            pl.BlockSpec((None, None, s, d), lambda bi, gi, i: (bi, gi, 0, 0)),
            pl.BlockSpec((None, r * tq, 1), lambda bi, gi, i: (gi, 0, 0)),
            pl.BlockSpec(memory_space=pl.ANY),
        ],
        out_specs=pl.BlockSpec((None, None, r, tq, d), lambda bi, gi, i: (bi, gi, 0, i, 0)),
        out_shape=jax.ShapeDtypeStruct((b, g, r, s, d), MXU_DTYPE),
        input_output_aliases={4: 0},
        compiler_params=_cparams(("arbitrary", "arbitrary", "arbitrary")),
        name="window_attention",
    )(q5, k, v, sink_col, jnp.zeros((b, g, r, s, d), MXU_DTYPE))
    return o.reshape(b, hq, s, d)


def _toeplitz(a, n_q, n_k, off):
    length = a.shape[-1]
    lo = (n_q - 1) - off
    hi = (n_q + n_k - 1) - lo - length
    ap = jnp.pad(a, [(0, 0)] * (a.ndim - 1) + [(max(lo, 0), max(hi, 0))])
    ap = ap[..., max(-lo, 0):ap.shape[-1] - max(-hi, 0)]
    return jnp.stack([ap[..., n_q - 1 - q:n_q - 1 - q + n_k] for q in range(n_q)], axis=-2)


def _na_tables(rpb, rows):
    w = GRID_W
    kh = min(NA_ROWS, rows)
    configs = [(0, 0), (NA_QROWS, NA_QROWS - NA_ROWS // 2), (rows - NA_QROWS, rows - NA_KROWS)]
    qi = jnp.arange(NA_QROWS, dtype=jnp.int32)[:, None, None, None]
    qc = jnp.arange(w, dtype=jnp.int32)[None, :, None, None]
    kj = jnp.arange(NA_KROWS, dtype=jnp.int32)[None, None, :, None]
    kc = jnp.arange(w, dtype=jnp.int32)[None, None, None, :]
    full = (NA_QROWS, w, NA_KROWS, w)
    flat = (NA_QROWS * w, NA_KROWS * w)
    by_col = jnp.moveaxis(_toeplitz(rpb.astype(jnp.float32), w, w, NA_COLS - 1), 1, -1)
    tabs = []
    for r0, kr0 in configs:
        qr = r0 + qi
        kr = kr0 + kj
        rs = jnp.clip(qr - kh // 2, 0, rows - kh)
        cs = jnp.clip(qc - NA_COLS // 2, 0, w - NA_COLS)
        valid = (kr >= rs) & (kr < rs + kh) & (kc >= cs) & (kc < cs + NA_COLS)
        valid = jnp.broadcast_to(valid, full).reshape(flat)
        bias = _toeplitz(by_col, NA_QROWS, NA_KROWS, kr0 - r0 + NA_ROWS - 1)
        bias = jnp.transpose(bias, (0, 3, 1, 4, 2)).reshape((-1,) + flat)
        tabs.append(jnp.where(valid[None], bias, NEG_INF))
    return jnp.stack(tabs, axis=1)


def _na_kernel(q_ref, k_ref, v_ref, tab_ref, init_ref, o_ref, *, tq, tkw, n_lat, n_ctx, nq):
    del init_ref
    i = pl.program_id(2)
    kind = jnp.where(i == 0, 0, jnp.where(i == nq - 1, 2, 1))
    start = pl.multiple_of(jnp.clip(i * tq - (NA_ROWS // 2) * GRID_W, 0, n_lat - tkw), GRID_W)
    q = q_ref[...]
    k_loc = k_ref[pl.ds(start, tkw), :]
    v_loc = v_ref[pl.ds(start, tkw), :]
    k_ctx = k_ref[n_lat:n_lat + n_ctx, :]
    v_ctx = v_ref[n_lat:n_lat + n_ctx, :]
    s_loc = _dot_nt(q, k_loc) + tab_ref[kind]
    s_ctx = _dot_nt(q, k_ctx)
    m = jnp.maximum(jnp.max(s_loc, axis=-1, keepdims=True), jnp.max(s_ctx, axis=-1, keepdims=True))
    p_loc = jnp.exp(s_loc - m)
    p_ctx = jnp.exp(s_ctx - m)
    denom = jnp.sum(p_loc, axis=-1, keepdims=True) + jnp.sum(p_ctx, axis=-1, keepdims=True)
    o = (_dot(p_loc.astype(MXU_DTYPE), v_loc) + _dot(p_ctx.astype(MXU_DTYPE), v_ctx)) / denom
    o_ref[...] = o.astype(o_ref.dtype)


def _neighbourhood_attention(q, k, v, tables, n_lat, n_ctx):
    b, hq, s, d = q.shape
    tq = NA_QROWS * GRID_W
    tkw = NA_KROWS * GRID_W
    nq = n_lat // tq
    return pl.pallas_call(
        functools.partial(_na_kernel, tq=tq, tkw=tkw, n_lat=n_lat, n_ctx=n_ctx, nq=nq),
        grid=(b, hq, nq),
        in_specs=[
            pl.BlockSpec((None, None, tq, d), lambda bi, h, i: (bi, h, i, 0)),
            pl.BlockSpec((None, None, s, d), lambda bi, h, i: (bi, h, 0, 0)),
            pl.BlockSpec((None, None, s, d), lambda bi, h, i: (bi, h, 0, 0)),
            pl.BlockSpec((None, 3, tq, tkw), lambda bi, h, i: (h, 0, 0, 0)),
            pl.BlockSpec(memory_space=pl.ANY),
        ],
        out_specs=pl.BlockSpec((None, None, tq, d), lambda bi, h, i: (bi, h, i, 0)),
        out_shape=jax.ShapeDtypeStruct((b, hq, s, d), MXU_DTYPE),
        input_output_aliases={4: 0},
        compiler_params=_cparams(("arbitrary", "arbitrary", "arbitrary")),
        name="neighbourhood_attention",
    )(q, k, v, tables, jnp.zeros((b, hq, s, d), MXU_DTYPE))


def _outproj_kernel(o1_ref, o2_ref, w_ref, x_ref, mod_ref, rw_ref, rb_ref,
                    xo_ref, h_ref, lg_ref, *, tm, n_lat, n_heads, transposed):
    row0 = pl.program_id(1) * tm
    if transposed:
        yt = jnp.concatenate([o1_ref[...].reshape(-1, tm), o2_ref[...].reshape(-1, tm)], axis=0)
        z = lax.dot_general(yt, w_ref[...], (((0,), (0,)), ((), ())), preferred_element_type=jnp.float32)
    else:
        y = jnp.concatenate([o1_ref[h] for h in range(n_heads)] + [o2_ref[h] for h in range(n_heads)], axis=-1)
        z = _dot(y, w_ref[...])
    rows = row0 + lax.broadcasted_iota(jnp.int32, (tm, 1), 0)
    gate = jnp.where(rows >= n_lat, mod_ref[6:7, :], mod_ref[4:5, :])
    xn = x_ref[...] + gate * z
    xo_ref[...] = xn
    h = _modulated_norm(xn, mod_ref, row0, n_lat, 0, 1, 2, 3)
    _store_row_tiles(h_ref, h)
    lg_ref[...] = jnp.dot(h, rw_ref[...], preferred_element_type=jnp.float32,
                          precision=lax.Precision.HIGHEST) + rb_ref[...]


def _outproj(o1, o2, w_out, xs, modv, router_w, router_b, n_lat, transposed):
    b, s, dm = xs.shape
    tm = ROW_TILE
    nh = o1.shape[1]
    nl = router_w.shape[1]
    if transposed:
        o_spec = pl.BlockSpec((None, nh, o1.shape[2], tm), lambda bi, j: (bi, 0, 0, j))
    else:
        o_spec = pl.BlockSpec((None, nh, tm, o1.shape[3]), lambda bi, j: (bi, 0, j, 0))
    return pl.pallas_call(
        functools.partial(_outproj_kernel, tm=tm, n_lat=n_lat, n_heads=nh, transposed=transposed),
        grid=(b, s // tm),
        in_specs=[
            o_spec,
            o_spec,
            pl.BlockSpec(w_out.shape, lambda bi, j: (0, 0)),
            pl.BlockSpec((None, tm, dm), lambda bi, j: (bi, j, 0)),
            pl.BlockSpec((None, 8, dm), lambda bi, j: (bi, 0, 0)),
            pl.BlockSpec(router_w.shape, lambda bi, j: (0, 0)),
            pl.BlockSpec(router_b.shape, lambda bi, j: (0, 0)),
        ],
        out_specs=[pl.BlockSpec((None, tm, dm), lambda bi, j: (bi, j, 0)),
                   pl.BlockSpec((None, tm * ROW_TILES, LANES), lambda bi, j: (bi, j, 0)),
                   pl.BlockSpec((None, tm, nl), lambda bi, j: (bi, j, 0))],
        out_shape=[jax.ShapeDtypeStruct((b, s, dm), jnp.float32),
                   jax.ShapeDtypeStruct((b, s * ROW_TILES, LANES), jnp.float32),
                   jax.ShapeDtypeStruct((b, s, nl), jnp.float32)],
        input_output_aliases={3: 0},
        compiler_params=_cparams(("arbitrary", "arbitrary")),
        name="outproj_residual_norm_router",
    )(o1, o2, w_out, xs, modv, router_w, router_b)


def _expert_kernel(be_ref, tok0_ref, tokn_ref, h_hbm, win_ref, bin_ref, wout_ref, bout_ref,
                   y_ref, xbuf, sems, *, blk, nb):
    del be_ref
    i = pl.program_id(0)
    slot = lax.rem(i, 2)
    rt = ROW_TILES

    def gather(tok_ref, dst_slot):
        for rr in range(blk):
            src = pl.multiple_of(tok_ref[0, rr], rt)
            pltpu.make_async_copy(h_hbm.at[pl.ds(src, rt)], xbuf.at[dst_slot, pl.ds(rr * rt, rt)],
                                  sems.at[dst_slot]).start()

    @pl.when(i == 0)
    def _():
        gather(tok0_ref, 0)

    @pl.when(i + 1 < nb)
    def _():
        gather(tokn_ref, 1 - slot)

    pltpu.make_async_copy(h_hbm.at[pl.ds(0, blk * rt)], xbuf.at[slot], sems.at[slot]).wait()

    x = _load_row_tiles(xbuf.at[slot], blk).astype(MXU_DTYPE)
    u = _dot(x, win_ref[...]) + bin_ref[...]
    glu = jnp.minimum(u[:, :D_EXPERT], SWIGLU_LIMIT)
    lin = jnp.clip(u[:, D_EXPERT:], -SWIGLU_LIMIT, SWIGLU_LIMIT)
    act = glu * jax.nn.sigmoid(SWIGLU_ALPHA * glu) * (lin + 1.0)
    _store_row_tiles(y_ref, _dot(act.astype(MXU_DTYPE), wout_ref[...]) + bout_ref[...])


def _experts(h_tiles, blk_expert, tok_buf, w_in, b_in, w_out, b_out):
    dm = w_in.shape[1]
    blk = MOE_ROWS
    rt = ROW_TILES
    nb = blk_expert.shape[0]
    de2 = w_in.shape[-1]
    tok3 = tok_buf.reshape(nb, 1, blk)
    grid_spec = pltpu.PrefetchScalarGridSpec(
        num_scalar_prefetch=1,
        grid=(nb,),
        in_specs=[
            pl.BlockSpec((None, 1, blk), lambda i, be: (0, 0, 0), memory_space=pltpu.SMEM),
            pl.BlockSpec((None, 1, blk), lambda i, be: (jnp.minimum(i + 1, nb - 1), 0, 0),
                         memory_space=pltpu.SMEM),
            pl.BlockSpec(memory_space=pl.ANY),
            pl.BlockSpec((None, dm, de2), lambda i, be: (be[i], 0, 0)),
            pl.BlockSpec((None, 1, de2), lambda i, be: (be[i], 0, 0)),
            pl.BlockSpec((None, de2 // 2, dm), lambda i, be: (be[i], 0, 0)),
            pl.BlockSpec((None, 1, dm), lambda i, be: (be[i], 0, 0)),
        ],
        out_specs=pl.BlockSpec((blk * rt, LANES), lambda i, be: (i, 0)),
        scratch_shapes=[pltpu.VMEM((2, blk * rt, LANES), jnp.float32), pltpu.SemaphoreType.DMA((2,))],
    )
    return pl.pallas_call(
        functools.partial(_expert_kernel, blk=blk, nb=nb),
        grid_spec=grid_spec,
        out_shape=jax.ShapeDtypeStruct((nb * blk * rt, LANES), jnp.float32),
        compiler_params=_cparams(("arbitrary",)),
        name="routed_experts",
    )(blk_expert, tok3, tok3, h_tiles, w_in, b_in.reshape(N_EXPERTS, 1, de2), w_out,
      b_out.reshape(N_EXPERTS, 1, dm))


def _combine_kernel(pos0_ref, posn_ref, y_hbm, x_ref, mod_ref, w_ref, o_ref, gbuf, sems, *, tc, nsteps, n_lat,
                    steps_per_batch):
    i = pl.program_id(0)
    slot = lax.rem(i, 2)

    rt = ROW_TILES

    def gather(pos_ref, dst_slot):
        def issue(t, carry):
            for kk in range(TOP_K):
                src = pl.multiple_of(pos_ref[0, t * TOP_K + kk], rt)
                dst = pl.multiple_of(t * rt, rt)
                pltpu.make_async_copy(y_hbm.at[pl.ds(src, rt)], gbuf.at[dst_slot, kk, pl.ds(dst, rt)],
                                      sems.at[dst_slot]).start()
            return carry
        lax.fori_loop(0, tc, issue, 0, unroll=8)

    @pl.when(i == 0)
    def _():
        gather(pos0_ref, 0)

    @pl.when(i + 1 < nsteps)
    def _():
        gather(posn_ref, 1 - slot)

    for kk in range(TOP_K):
        pltpu.make_async_copy(y_hbm.at[pl.ds(0, tc * rt)], gbuf.at[slot, kk], sems.at[slot]).wait()

    w = w_ref[...]
    y = _load_row_tiles(gbuf.at[slot, 0], tc) * w[:, 0:1]
    for kk in range(1, TOP_K):
        y = y + _load_row_tiles(gbuf.at[slot, kk], tc) * w[:, kk:kk + 1]
    row0 = lax.rem(i, steps_per_batch) * tc
    rows = row0 + lax.broadcasted_iota(jnp.int32, (tc, 1), 0)
    gate = jnp.where(rows >= n_lat, mod_ref[7:8, :], mod_ref[5:6, :])
    o_ref[...] = x_ref[...] + gate * y


def _combine(y_sorted, pos, weights, xs_flat, modv, s, n_lat):
    n, dm = xs_flat.shape
    tc = COMBINE_ROWS
    nsteps = n // tc
    spb = s // tc
    pos3 = pos.reshape(nsteps, 1, tc * TOP_K)
    return pl.pallas_call(
        functools.partial(_combine_kernel, tc=tc, nsteps=nsteps, n_lat=n_lat, steps_per_batch=spb),
        grid=(nsteps,),
        in_specs=[
            pl.BlockSpec((None, 1, tc * TOP_K), lambda i: (0, 0, 0), memory_space=pltpu.SMEM),
            pl.BlockSpec((None, 1, tc * TOP_K), lambda i: (jnp.minimum(i + 1, nsteps - 1), 0, 0),
                         memory_space=pltpu.SMEM),
            pl.BlockSpec(memory_space=pl.ANY),
            pl.BlockSpec((tc, dm), lambda i: (i, 0)),
            pl.BlockSpec((None, 8, dm), lambda i: (i // spb, 0, 0)),
            pl.BlockSpec((tc, TOP_K), lambda i: (i, 0)),
        ],
        out_specs=pl.BlockSpec((tc, dm), lambda i: (i, 0)),
        out_shape=jax.ShapeDtypeStruct((n, dm), jnp.float32),
        scratch_shapes=[pltpu.VMEM((2, TOP_K, tc * ROW_TILES, LANES), jnp.float32), pltpu.SemaphoreType.DMA((2,))],
        input_output_aliases={3: 0},
        compiler_params=_cparams(("arbitrary",)),
        name="expert_combine",
    )(pos3, pos3, y_sorted, xs_flat, modv, weights)


def _final_norm_kernel(x_ref, g_ref, o_ref):
    x = x_ref[...]
    o_ref[...] = x * lax.rsqrt(jnp.mean(x * x, axis=-1, keepdims=True) + NORM_EPS) * g_ref[...]


def _final_norm(xs, g, n_lat):
    b, s, dm = xs.shape
    tm = 512
    return pl.pallas_call(
        _final_norm_kernel,
        grid=(b, n_lat // tm),
        in_specs=[pl.BlockSpec((None, tm, dm), lambda bi, j: (bi, j, 0)),
                  pl.BlockSpec((1, dm), lambda bi, j: (0, 0))],
        out_specs=pl.BlockSpec((None, tm, dm), lambda bi, j: (bi, j, 0)),
        out_shape=jax.ShapeDtypeStruct((b, n_lat, dm), jnp.float32),
        compiler_params=_cparams(("arbitrary", "arbitrary")),
        name="final_norm",
    )(xs, g.reshape(1, dm))


def _swap_perm(width, dim):
    j = np.arange(width)
    half = dim // 2
    return np.where((j % dim) < half, j + half, j - half)


def _rope_tables(n_lat, n_ctx, dim):
    t = jnp.arange(n_lat, dtype=jnp.int32)
    row = (t // GRID_W).astype(jnp.float32)
    col = (t % GRID_W).astype(jnp.float32)
    quarter = dim // 4
    inv_freq = ROPE_BASE ** (-jnp.arange(quarter, dtype=jnp.float32) / quarter)
    ang = jnp.concatenate([row[:, None] * inv_freq, col[:, None] * inv_freq], axis=-1)
    cos, sin = jnp.cos(ang), jnp.sin(ang)
    reps = LANES // dim
    cos_t = jnp.tile(jnp.concatenate([cos, cos], axis=-1), (1, reps))
    sin_t = jnp.tile(jnp.concatenate([-sin, sin], axis=-1), (1, reps))
    cos_t = jnp.concatenate([cos_t, jnp.ones((n_ctx, LANES), jnp.float32)], axis=0)
    sin_t = jnp.concatenate([sin_t, jnp.zeros((n_ctx, LANES), jnp.float32)], axis=0)
    return cos_t, sin_t


def _mod_rows(mod_l, norm_g, b, first):
    lat, ctx = mod_l[:b], mod_l[b]
    a_lat = norm_g[None] * (1.0 + lat[:, first + 1])
    b_lat = lat[:, first]
    a_ctx = jnp.broadcast_to(norm_g * (1.0 + ctx[first + 1]), a_lat.shape)
    b_ctx = jnp.broadcast_to(ctx[first], a_lat.shape)
    g_mix_ctx = jnp.broadcast_to(ctx[2], a_lat.shape)
    g_ffn_ctx = jnp.broadcast_to(ctx[5], a_lat.shape)
    return jnp.stack([a_lat, b_lat, a_ctx, b_ctx, lat[:, 2], lat[:, 5], g_mix_ctx, g_ffn_ctx], axis=1)


def _route(logits, blk):
    n = logits.shape[0]
    nk = n * TOP_K
    top_val, top_idx = lax.top_k(logits, TOP_K)
    gate = jax.nn.softmax(top_val, axis=-1)
    expert = top_idx.reshape(-1)
    onehot = (expert[:, None] == jnp.arange(N_EXPERTS, dtype=jnp.int32)[None, :]).astype(jnp.int32)
    csum = jnp.cumsum(onehot, axis=0)
    rank = jnp.take_along_axis(csum, expert[:, None], axis=1)[:, 0] - 1
    sizes = csum[-1]
    padded = (sizes + blk - 1) // blk * blk
    pends = jnp.cumsum(padded)
    pstarts = pends - padded
    dest = pstarts[expert] + rank
    nb = -(-(nk + N_EXPERTS * (blk - 1)) // blk)
    blk_expert = jnp.minimum(jnp.searchsorted(pends, jnp.arange(nb, dtype=jnp.int32) * blk, side='right'),
                             N_EXPERTS - 1).astype(jnp.int32)
    order = jnp.argsort(expert)
    starts = jnp.cumsum(sizes) - sizes
    slot = jnp.arange(nb * blk, dtype=jnp.int32)
    slot_expert = jnp.repeat(blk_expert, blk)
    slot_rank = slot - pstarts[slot_expert]
    src = jnp.minimum(starts[slot_expert] + slot_rank, nk - 1)
    tok_buf = jnp.where(slot_rank < sizes[slot_expert], order[src] // TOP_K, 0).astype(jnp.int32)
    return blk_expert, tok_buf, gate, dest.astype(jnp.int32)


def kernel(x, c, ctx, c_ctx, mod_w, mod_b, norm_mix, norm_ffn, ab_w_in, ab_w_out, a_sink, b_rpb,
           cd_w_in, c_q_norm, c_w_q_b, c_kv_norm, c_w_kv_b, d_q_norm, d_k_norm, cd_w_out,
           router_w, router_b, exp_w_in, exp_b_in, exp_w_out, exp_b_out, final_norm):
    b, t, dm = x.shape
    assert dm == ROW_TILES * LANES
    n_ctx = ctx.shape[1]
    s = t + n_ctx
    depth = mod_w.shape[0]
    rows = t // GRID_W
    d = HEAD_DIM
    f32 = jnp.float32

    xs = jnp.concatenate([x, ctx], axis=1)

    cond = jnp.zeros((8, dm), f32).at[:b].set(c).at[b].set(c_ctx)
    mod_all = _modulation(cond, mod_w, mod_b)[:, :b + 1].reshape(depth, b + 1, 6, dm)

    cos_t, sin_t = _rope_tables(t, n_ctx, d)
    cosm_t, sinm_t = _rope_tables(t, n_ctx, C_ROPE)

    router_w_p = jnp.zeros((depth, dm, LANES), f32).at[:, :, :N_EXPERTS].set(router_w)
    router_b_p = jnp.zeros((depth, 1, LANES), f32).at[:, 0, :N_EXPERTS].set(router_b)

    p64_512 = _swap_perm(512, d)
    p64_128 = _swap_perm(128, d)
    p32 = _swap_perm(C_ROPE, C_ROPE)

    for layer in range(depth):
        i = layer // 2
        modv_mix = _mod_rows(mod_all[layer], norm_mix[layer], b, 0)
        modv_ffn = _mod_rows(mod_all[layer], norm_ffn[layer], b, 3)
        if layer % 2 == 0:
            w = ab_w_in[i]
            w_ext = jnp.concatenate([w, w[:, 0:512][:, p64_512], w[:, 512:640][:, p64_128]], axis=1).astype(MXU_DTYPE)
            qa, ka, va, qb, kb, vb = _proj_ab(xs, modv_mix, w_ext, cos_t, sin_t, t)
            sink_gr = a_sink[i].astype(f32).reshape(A_KV_HEADS, A_HEADS // A_KV_HEADS)
            r = A_HEADS // A_KV_HEADS
            sink_win = jnp.repeat(sink_gr, WIN_Q, axis=1).reshape(A_KV_HEADS, r * WIN_Q, 1)
            sink_ctx = jnp.repeat(sink_gr, n_ctx, axis=1).reshape(A_KV_HEADS, r * n_ctx, 1)
            o1 = _window_attention(qa, ka, va, sink_win, t, n_ctx)
            o1 = _flash(qa, ka, va, groups=A_KV_HEADS, tq=n_ctx, tk=n_ctx, q_row0=t, n_q=n_ctx,
                        kv_row0=t, n_kv=n_ctx, sink=sink_ctx, out=o1)
            tables = _na_tables(b_rpb[i], rows)
            o2 = _neighbourhood_attention(qb, kb, vb, tables, t, n_ctx)
            o2 = _flash(qb, kb, vb, groups=B_HEADS, tq=n_ctx, tk=n_ctx, q_row0=t, n_q=n_ctx,
                        kv_row0=t, n_kv=n_ctx, out=o2)
            w_out = ab_w_out[i].astype(MXU_DTYPE)
        else:
            w = cd_w_in[i]
            base = C_Q_RANK + C_KV_RANK
            o_qd, o_kd, o_vd = base + C_ROPE, base + C_ROPE + 512, base + C_ROPE + 640
            wqd, wkd = w[:, o_qd:o_qd + 512], w[:, o_kd:o_kd + 128]
            gq = jnp.tile(d_q_norm[i].astype(f32), D_HEADS)
            gk = jnp.tile(d_k_norm[i].astype(f32), D_KV_HEADS)
            krope = w[:, base:base + C_ROPE]
            pad = jnp.zeros((dm, LANES - C_ROPE), f32)
            w1 = jnp.concatenate([
                w[:, 0:base], wqd, wkd, w[:, o_vd:o_vd + 128],
                (wqd * gq[None])[:, p64_512], (wkd * gk[None])[:, p64_128],
                krope, pad, krope[:, p32], pad], axis=1).astype(MXU_DTYPE)
            dq = C_NOPE + C_ROPE
            wq = c_w_q_b[i].reshape(C_Q_RANK, C_HEADS, dq)
            wq_nope = wq[:, :, :C_NOPE].reshape(C_Q_RANK, C_HEADS * C_NOPE)
            wq_rope = wq[:, :, C_NOPE:].reshape(C_Q_RANK, C_HEADS * C_ROPE)
            wq2 = jnp.concatenate([wq_nope, wq_rope, wq_rope[:, _swap_perm(C_HEADS * C_ROPE, C_ROPE)]],
                                  axis=1).astype(MXU_DTYPE)
            wkv = c_w_kv_b[i].reshape(C_KV_RANK, C_HEADS, C_NOPE + C_V)
            wkv2 = jnp.concatenate([wkv[:, :, :C_NOPE].reshape(C_KV_RANK, -1),
                                    wkv[:, :, C_NOPE:].reshape(C_KV_RANK, -1)], axis=1).astype(MXU_DTYPE)
            qc, kc, vc, qd, kd, vd = _proj_cd(
                xs, modv_mix, w1, wq2, wkv2, c_q_norm[i].astype(f32).reshape(1, -1),
                c_kv_norm[i].astype(f32).reshape(1, -1), gq.reshape(1, -1), gk.reshape(1, -1),
                cos_t, sin_t, cosm_t, sinm_t, t)
            o1 = _flash_t(qc, kc, vc, groups=C_HEADS, tq=FLASH_M, tk=FLASH_TK, q_row0=0, n_q=t, kv_row0=0, n_kv=s,
                          out=jnp.zeros((b, C_HEADS, C_V, s), MXU_DTYPE))
            o1 = _flash_t(qc, kc, vc, groups=C_HEADS, tq=n_ctx, tk=n_ctx, q_row0=t, n_q=n_ctx,
                          kv_row0=t, n_kv=n_ctx, out=o1)
            rd = D_HEADS // D_KV_HEADS
            o2 = _flash_t(qd, kd, vd, groups=D_KV_HEADS, tq=FLASH_M // rd, tk=FLASH_TK, q_row0=0, n_q=t,
                          kv_row0=0, n_kv=s, out=jnp.zeros((b, D_HEADS, d, s), MXU_DTYPE))
            o2 = _flash_t(qd, kd, vd, groups=D_KV_HEADS, tq=n_ctx, tk=n_ctx, q_row0=t, n_q=n_ctx,
                          kv_row0=t, n_kv=n_ctx, out=o2)
            w_out = cd_w_out[i].astype(MXU_DTYPE)

        xs, h2, logits = _outproj(o1, o2, w_out, xs, modv_ffn, router_w_p[layer], router_b_p[layer], t,
                                  transposed=layer % 2 == 1)

        n = b * s
        blk_expert, tok_buf, weights, pos = _route(logits.reshape(n, LANES)[:, :N_EXPERTS], MOE_ROWS)
        y_sorted = _experts(h2.reshape(n * ROW_TILES, LANES), blk_expert, tok_buf * ROW_TILES,
                            exp_w_in[layer].astype(MXU_DTYPE), exp_b_in[layer].astype(f32),
                            exp_w_out[layer].astype(MXU_DTYPE), exp_b_out[layer].astype(f32))
        xs = _combine(y_sorted, pos * ROW_TILES, weights, xs.reshape(n, dm), modv_ffn, s, t).reshape(b, s, dm)

    return _final_norm(xs, final_norm.astype(f32), t)
```

```python
import functools

import jax
import jax.numpy as jnp
import numpy as np
from jax import lax
from jax.experimental import pallas as pl
from jax.experimental.pallas import tpu as pltpu

GRID_W = 64
HEAD_DIM = 64
ROPE_BASE = 10000.0
NORM_EPS = 1e-6
NEG_INF = -1e30
A_HEADS = 8
A_KV_HEADS = 2
A_WINDOW = 128
B_HEADS = 8
NA_ROWS = 8
NA_COLS = 16
C_HEADS = 8
C_Q_RANK = 768
C_KV_RANK = 256
C_NOPE = 64
C_ROPE = 32
C_V = 64
D_HEADS = 8
D_KV_HEADS = 2
N_EXPERTS = 32
TOP_K = 4
D_EXPERT = 1024
SWIGLU_LIMIT = 7.0
SWIGLU_ALPHA = 1.702
LOG2_E = 1.4426950408889634

LANES = 128
BF16_SUBLANES = 16
ROW_TILES = 8
VMEM_LIMIT_BYTES = 56 * 1024 * 1024

MXU_DTYPE = jnp.bfloat16

ROW_TILE = 640
MOE_ROWS = 256
COMBINE_ROWS = 256
NA_QROWS = 8
NA_KROWS = 16
WIN_Q = 256
FLASH_M = 1024
FLASH_TK = 640


def _cparams(sem, vmem=VMEM_LIMIT_BYTES):
    return pltpu.CompilerParams(dimension_semantics=sem, vmem_limit_bytes=vmem)


def _dot(a, b):
    return jnp.dot(a, b, preferred_element_type=jnp.float32)


def _dot_nt(a, b):
    return lax.dot_general(a, b, (((1,), (1,)), ((), ())), preferred_element_type=jnp.float32)


def _mod_kernel(c_ref, w_ref, b_ref, o_ref):
    c = c_ref[...]
    s = c * jax.nn.sigmoid(c)
    o_ref[...] = jnp.dot(s, w_ref[...], preferred_element_type=jnp.float32,
                         precision=lax.Precision.HIGHEST) + b_ref[...]


def _modulation(cond, mod_w, mod_b):
    depth, dm, n = mod_w.shape
    tn = 1536
    return pl.pallas_call(
        _mod_kernel,
        grid=(depth, n // tn),
        in_specs=[
            pl.BlockSpec((8, dm), lambda l, j: (0, 0)),
            pl.BlockSpec((None, dm, tn), lambda l, j: (l, 0, j)),
            pl.BlockSpec((None, 1, tn), lambda l, j: (l, 0, j)),
        ],
        out_specs=pl.BlockSpec((None, 8, tn), lambda l, j: (l, 0, j)),
        out_shape=jax.ShapeDtypeStruct((depth, 8, n), jnp.float32),
        compiler_params=_cparams(("arbitrary", "arbitrary")),
        name="adaln_modulation",
    )(cond, mod_w, mod_b.reshape(depth, 1, n))


def _modulated_norm(x, mod_ref, row0, n_lat, a_lat, b_lat, a_ctx, b_ctx):
    tm = x.shape[0]
    ms = jnp.mean(x * x, axis=-1, keepdims=True)
    xh = x * lax.rsqrt(ms + NORM_EPS)
    rows = row0 + lax.broadcasted_iota(jnp.int32, (tm, 1), 0)
    is_ctx = rows >= n_lat
    a = jnp.where(is_ctx, mod_ref[a_ctx:a_ctx + 1, :], mod_ref[a_lat:a_lat + 1, :])
    b = jnp.where(is_ctx, mod_ref[b_ctx:b_ctx + 1, :], mod_ref[b_lat:b_lat + 1, :])
    return xh * a + b


def _store_heads(o_ref, val, n_heads, width, lane0=0):
    for h in range(n_heads):
        o_ref[h, :, lane0:lane0 + width] = val[:, h * width:(h + 1) * width].astype(o_ref.dtype)


def _store_row_tiles(o_ref, val):
    rows = val.shape[0]
    for c in range(ROW_TILES):
        o_ref[pl.ds(c, rows, stride=ROW_TILES), :] = val[:, c * LANES:(c + 1) * LANES]


def _load_row_tiles(ref, rows):
    return jnp.concatenate([ref[pl.ds(c, rows, stride=ROW_TILES), :] for c in range(ROW_TILES)], axis=-1)


def _tile_lanes(t, reps):
    return t if reps == 1 else jnp.concatenate([t] * reps, axis=-1)


def _proj_ab_kernel(x_ref, mod_ref, w_ref, cos_ref, sin_ref,
                    qa_ref, ka_ref, va_ref, qb_ref, kb_ref, vb_ref, *, tm, n_lat):
    row0 = pl.program_id(1) * tm
    h = _modulated_norm(x_ref[...], mod_ref, row0, n_lat, 0, 1, 2, 3).astype(MXU_DTYPE)
    cos = cos_ref[...]
    sin = sin_ref[...]
    d = HEAD_DIM
    scale = d ** -0.5
    qa = _dot(h, w_ref[:, 0:512]) * _tile_lanes(cos, 4) + _dot(h, w_ref[:, 2304:2816]) * _tile_lanes(sin, 4)
    _store_heads(qa_ref, qa * scale, A_HEADS, d)
    ka = _dot(h, w_ref[:, 512:640]) * cos + _dot(h, w_ref[:, 2816:2944]) * sin
    _store_heads(ka_ref, ka, A_KV_HEADS, d)
    _store_heads(va_ref, _dot(h, w_ref[:, 640:768]), A_KV_HEADS, d)
    _store_heads(qb_ref, _dot(h, w_ref[:, 768:1280]) * scale, B_HEADS, d)
    _store_heads(kb_ref, _dot(h, w_ref[:, 1280:1792]), B_HEADS, d)
    _store_heads(vb_ref, _dot(h, w_ref[:, 1792:2304]), B_HEADS, d)


def _proj_ab(xs, modv, w_ext, cos_t, sin_t, n_lat):
    b, s, dm = xs.shape
    tm = ROW_TILE
    d = HEAD_DIM
    nw = w_ext.shape[1]

    def head_spec(nh):
        return pl.BlockSpec((None, nh, tm, d), lambda bi, j: (bi, 0, j, 0))

    def head_shape(nh):
        return jax.ShapeDtypeStruct((b, nh, s, d), MXU_DTYPE)

    return pl.pallas_call(
        functools.partial(_proj_ab_kernel, tm=tm, n_lat=n_lat),
        grid=(b, s // tm),
        in_specs=[
            pl.BlockSpec((None, tm, dm), lambda bi, j: (bi, j, 0)),
            pl.BlockSpec((None, 8, dm), lambda bi, j: (bi, 0, 0)),
            pl.BlockSpec((dm, nw), lambda bi, j: (0, 0)),
            pl.BlockSpec((tm, LANES), lambda bi, j: (j, 0)),
            pl.BlockSpec((tm, LANES), lambda bi, j: (j, 0)),
        ],
        out_specs=[head_spec(A_HEADS), head_spec(A_KV_HEADS), head_spec(A_KV_HEADS),
                   head_spec(B_HEADS), head_spec(B_HEADS), head_spec(B_HEADS)],
        out_shape=[head_shape(A_HEADS), head_shape(A_KV_HEADS), head_shape(A_KV_HEADS),
                   head_shape(B_HEADS), head_shape(B_HEADS), head_shape(B_HEADS)],
        compiler_params=_cparams(("arbitrary", "arbitrary")),
        name="proj_ab",
    )(xs, modv, w_ext, cos_t, sin_t)


def _proj_cd_kernel(x_ref, mod_ref, w1_ref, wq2_ref, wkv2_ref, qn_ref, kvn_ref, gq_ref, gk_ref,
                    cos_ref, sin_ref, cosm_ref, sinm_ref,
                    qc_ref, kc_ref, vc_ref, qd_ref, kd_ref, vd_ref, *, tm, n_lat):
    row0 = pl.program_id(1) * tm
    h = _modulated_norm(x_ref[...], mod_ref, row0, n_lat, 0, 1, 2, 3).astype(MXU_DTYPE)
    cos = cos_ref[...]
    sin = sin_ref[...]
    cosm = cosm_ref[...]
    sinm = sinm_ref[...]
    d = HEAD_DIM

    cq = _dot(h, w1_ref[:, 0:768])
    cqn = cq * lax.rsqrt(jnp.mean(cq * cq, axis=-1, keepdims=True) + NORM_EPS) * qn_ref[...]
    q2 = _dot(cqn.astype(MXU_DTYPE), wq2_ref[...])
    c_scale = (C_NOPE + C_ROPE) ** -0.5 * LOG2_E
    q_rope = (q2[:, 512:768] * _tile_lanes(cosm, 2) + q2[:, 768:1024] * _tile_lanes(sinm, 2)) * c_scale
    _store_heads(qc_ref, q2[:, 0:512] * c_scale, C_HEADS, C_NOPE)
    _store_heads(qc_ref, q_rope, C_HEADS, C_ROPE, lane0=C_NOPE)

    ckv = _dot(h, w1_ref[:, 768:1024])
    ckvn = ckv * lax.rsqrt(jnp.mean(ckv * ckv, axis=-1, keepdims=True) + NORM_EPS) * kvn_ref[...]
    kv2 = _dot(ckvn.astype(MXU_DTYPE), wkv2_ref[...])
    k_rope = (_dot(h, w1_ref[:, 2432:2560]) * cosm + _dot(h, w1_ref[:, 2560:2688]) * sinm)[:, 0:C_ROPE]
    _store_heads(kc_ref, kv2[:, 0:512], C_HEADS, C_NOPE)
    for hh in range(C_HEADS):
        kc_ref[hh, :, C_NOPE:C_NOPE + C_ROPE] = k_rope.astype(kc_ref.dtype)
    _store_heads(vc_ref, kv2[:, 512:1024], C_HEADS, C_V)

    def normed_rope(raw, swapped_gained, gain, n_heads, o_ref, scale):
        reps = n_heads * d // LANES
        val = raw * (_tile_lanes(cos, reps) * gain) + swapped_gained * _tile_lanes(sin, reps)
        for hh in range(n_heads):
            sl = raw[:, hh * d:(hh + 1) * d]
            r = lax.rsqrt(jnp.mean(sl * sl, axis=-1, keepdims=True) + NORM_EPS)
            o_ref[hh, :, :] = (val[:, hh * d:(hh + 1) * d] * (r * scale)).astype(o_ref.dtype)

    normed_rope(_dot(h, w1_ref[:, 1024:1536]), _dot(h, w1_ref[:, 1792:2304]), gq_ref[...], D_HEADS, qd_ref,
                d ** -0.5 * LOG2_E)
    normed_rope(_dot(h, w1_ref[:, 1536:1664]), _dot(h, w1_ref[:, 2304:2432]), gk_ref[...], D_KV_HEADS, kd_ref, 1.0)
    _store_heads(vd_ref, _dot(h, w1_ref[:, 1664:1792]), D_KV_HEADS, d)


def _proj_cd(xs, modv, w1, wq2, wkv2, qn, kvn, gq, gk, cos_t, sin_t, cosm_t, sinm_t, n_lat):
    b, s, dm = xs.shape
    tm = ROW_TILE
    dk_c = C_NOPE + C_ROPE

    def full(a):
        return pl.BlockSpec(a.shape, lambda bi, j: (0,) * a.ndim)

    def head_spec(nh, w):
        return pl.BlockSpec((None, nh, tm, w), lambda bi, j: (bi, 0, j, 0))

    def head_shape(nh, w):
        return jax.ShapeDtypeStruct((b, nh, s, w), MXU_DTYPE)

    tab = pl.BlockSpec((tm, LANES), lambda bi, j: (j, 0))
    return pl.pallas_call(
        functools.partial(_proj_cd_kernel, tm=tm, n_lat=n_lat),
        grid=(b, s // tm),
        in_specs=[
            pl.BlockSpec((None, tm, dm), lambda bi, j: (bi, j, 0)),
            pl.BlockSpec((None, 8, dm), lambda bi, j: (bi, 0, 0)),
            full(w1), full(wq2), full(wkv2), full(qn), full(kvn), full(gq), full(gk),
            tab, tab, tab, tab,
        ],
        out_specs=[head_spec(C_HEADS, dk_c), head_spec(C_HEADS, dk_c), head_spec(C_HEADS, C_V),
                   head_spec(D_HEADS, HEAD_DIM), head_spec(D_KV_HEADS, HEAD_DIM), head_spec(D_KV_HEADS, HEAD_DIM)],
        out_shape=[head_shape(C_HEADS, dk_c), head_shape(C_HEADS, dk_c), head_shape(C_HEADS, C_V),
                   head_shape(D_HEADS, HEAD_DIM), head_shape(D_KV_HEADS, HEAD_DIM), head_shape(D_KV_HEADS, HEAD_DIM)],
        compiler_params=_cparams(("arbitrary", "arbitrary")),
        name="proj_cd",
    )(xs, modv, w1, wq2, wkv2, qn, kvn, gq, gk, cos_t, sin_t, cosm_t, sinm_t)


def _flash_kernel(*refs, r, tq, tk, nk, has_sink):
    if has_sink:
        q_ref, k_ref, v_ref, sink_ref, o_ref, m_sc, l_sc, acc_sc = refs
    else:
        q_ref, k_ref, v_ref, o_ref, m_sc, l_sc, acc_sc = refs
    m_rows = r * tq
    q = q_ref[...].reshape(m_rows, q_ref.shape[-1])
    if has_sink:
        m_sc[...] = sink_ref[...]
        l_sc[...] = jnp.ones_like(l_sc)
    else:
        m_sc[...] = jnp.full_like(m_sc, NEG_INF)
        l_sc[...] = jnp.zeros_like(l_sc)
    acc_sc[...] = jnp.zeros_like(acc_sc)

    def body(j, carry):
        k0 = pl.multiple_of(j * tk, tk)
        k = k_ref[pl.ds(k0, tk), :]
        v = v_ref[pl.ds(k0, tk), :]
        s = _dot_nt(q, k)
        m_prev = m_sc[...]
        m_new = jnp.maximum(m_prev, jnp.max(s, axis=-1, keepdims=True))
        alpha = jnp.exp(m_prev - m_new)
        p = jnp.exp(s - m_new)
        l_sc[...] = alpha * l_sc[...] + jnp.sum(p, axis=-1, keepdims=True)
        acc_sc[...] = alpha * acc_sc[...] + _dot(p.astype(MXU_DTYPE), v)
        m_sc[...] = m_new
        return carry

    lax.fori_loop(0, nk, body, 0)
    o = acc_sc[...] / l_sc[...]
    o_ref[...] = o.reshape(o_ref.shape).astype(o_ref.dtype)


def _flash(q, k, v, *, groups, tq, tk, q_row0, n_q, kv_row0, n_kv, sink=None, out=None):
    b, hq, s, dk = q.shape
    dv = v.shape[-1]
    r = hq // groups
    q5 = q.reshape(b, groups, r, s, dk)
    nk = n_kv // tk
    has_sink = sink is not None
    m_rows = r * tq
    in_specs = [
        pl.BlockSpec((None, None, r, tq, dk), lambda bi, g, i: (bi, g, 0, q_row0 // tq + i, 0)),
        pl.BlockSpec((None, None, n_kv, dk), lambda bi, g, i: (bi, g, kv_row0 // n_kv, 0)),
        pl.BlockSpec((None, None, n_kv, dv), lambda bi, g, i: (bi, g, kv_row0 // n_kv, 0)),
    ]
    args = [q5, k, v]
    if has_sink:
        in_specs.append(pl.BlockSpec((None, m_rows, 1), lambda bi, g, i: (g, 0, 0)))
        args.append(sink)
    aliases = {}
    if out is not None:
        in_specs.append(pl.BlockSpec(memory_space=pl.ANY))
        args.append(out.reshape(b, groups, r, s, dv))
        aliases = {len(args) - 1: 0}
    kern = functools.partial(_flash_kernel, r=r, tq=tq, tk=tk, nk=nk, has_sink=has_sink)
    if out is not None:
        inner = kern
        kern = lambda *refs: inner(*refs[:len(args) - 1], *refs[len(args):])
    o = pl.pallas_call(
        kern,
        grid=(b, groups, n_q // tq),
        in_specs=in_specs,
        out_specs=pl.BlockSpec((None, None, r, tq, dv), lambda bi, g, i: (bi, g, 0, q_row0 // tq + i, 0)),
        out_shape=jax.ShapeDtypeStruct((b, groups, r, s, dv), MXU_DTYPE),
        scratch_shapes=[pltpu.VMEM((m_rows, 1), jnp.float32), pltpu.VMEM((m_rows, 1), jnp.float32),
                        pltpu.VMEM((m_rows, dv), jnp.float32)],
        input_output_aliases=aliases,
        compiler_params=_cparams(("arbitrary", "arbitrary", "arbitrary")),
        name="dense_attention",
    )(*args)
    return o.reshape(b, hq, s, dv)


def _flash_t_kernel(q_ref, k_ref, vt_ref, init_ref, o_ref, m_sc, acc_sc, sa_sc, sb_sc, *, r, tq, tk, nk, dv):
    del init_ref
    m_cols = r * tq
    q = q_ref[...].reshape(m_cols, q_ref.shape[-1])
    m_sc[...] = jnp.full_like(m_sc, NEG_INF)
    acc_sc[...] = jnp.zeros_like(acc_sc)

    def scores(j, s_ref):
        k0 = pl.multiple_of(j * tk, tk)
        s_ref[...] = _dot_nt(k_ref[pl.ds(k0, tk), :], q)

    def update(j, s_ref):
        st = s_ref[...]
        m_prev = m_sc[...]
        m_new = jnp.maximum(m_prev, jnp.max(st, axis=0, keepdims=True))
        alpha = jnp.exp2(m_prev - m_new)
        pt = jnp.exp2(st - m_new)
        acc_sc[...] = alpha * acc_sc[...] + _dot(vt_ref[j], pt.astype(MXU_DTYPE))
        m_sc[...] = m_new

    scores(0, sa_sc)

    def body(jj, carry):
        j = 2 * jj
        scores(j + 1, sb_sc)
        update(j, sa_sc)
        scores(j + 2, sa_sc)
        update(j + 1, sb_sc)
        return carry

    lax.fori_loop(0, (nk - 1) // 2, body, 0)
    if nk % 2 == 0:
        scores(nk - 1, sb_sc)
        update(nk - 2, sa_sc)
        update(nk - 1, sb_sc)
    else:
        update(nk - 1, sa_sc)
    o = acc_sc[0:dv, :] / acc_sc[dv:dv + 1, :]
    for rr in range(r):
        o_ref[rr] = o[:, rr * tq:(rr + 1) * tq].astype(o_ref.dtype)


def _flash_t(q, k, v, *, groups, tq, tk, q_row0, n_q, kv_row0, n_kv, out):
    b, hq, s, dk = q.shape
    dv = v.shape[-1]
    r = hq // groups
    q5 = q.reshape(b, groups, r, s, dk)
    nk = n_kv // tk
    vt = jnp.swapaxes(lax.slice_in_dim(v, kv_row0, kv_row0 + n_kv, axis=2).reshape(b, groups, nk, tk, dv), 3, 4)
    dva = dv + BF16_SUBLANES
    vt = jnp.concatenate([vt, jnp.ones((b, groups, nk, BF16_SUBLANES, tk), vt.dtype)], axis=3)
    m_cols = r * tq
    o = pl.pallas_call(
        functools.partial(_flash_t_kernel, r=r, tq=tq, tk=tk, nk=nk, dv=dv),
        grid=(b, groups, n_q // tq),
        in_specs=[
            pl.BlockSpec((None, None, r, tq, dk), lambda bi, g, i: (bi, g, 0, q_row0 // tq + i, 0)),
            pl.BlockSpec((None, None, n_kv, dk), lambda bi, g, i: (bi, g, kv_row0 // n_kv, 0)),
            pl.BlockSpec((None, None, nk, dva, tk), lambda bi, g, i: (bi, g, 0, 0, 0)),
            pl.BlockSpec(memory_space=pl.ANY),
        ],
        out_specs=pl.BlockSpec((None, None, r, dv, tq), lambda bi, g, i: (bi, g, 0, 0, q_row0 // tq + i)),
        out_shape=jax.ShapeDtypeStruct((b, groups, r, dv, s), MXU_DTYPE),
        scratch_shapes=[pltpu.VMEM((1, m_cols), jnp.float32),
                        pltpu.VMEM((dva, m_cols), jnp.float32),
                        pltpu.VMEM((tk, m_cols), jnp.float32), pltpu.VMEM((tk, m_cols), jnp.float32)],
        input_output_aliases={3: 0},
        compiler_params=_cparams(("arbitrary", "arbitrary", "arbitrary")),
        name="dense_attention_t",
    )(q5, k, vt, out.reshape(b, groups, r, dv, s))
    return o.reshape(b, hq, dv, s)


def _window_kernel(q_ref, k_ref, v_ref, sink_ref, init_ref, o_ref, *, r, tq, n_lat, n_ctx):
    del init_ref
    i = pl.program_id(2)
    span = tq + 2 * A_WINDOW
    d = q_ref.shape[-1]
    q = q_ref[...].reshape(r * tq, d)
    start = pl.multiple_of(jnp.clip(i * tq - A_WINDOW, 0, n_lat - span), A_WINDOW)
    k_loc = k_ref[pl.ds(start, span), :]
    v_loc = v_ref[pl.ds(start, span), :]
    k_ctx = k_ref[n_lat:n_lat + n_ctx, :]
    v_ctx = v_ref[n_lat:n_lat + n_ctx, :]
    qpos = i * tq + lax.rem(lax.broadcasted_iota(jnp.int32, (r * tq, span), 0), tq)
    kpos = start + lax.broadcasted_iota(jnp.int32, (r * tq, span), 1)
    ok = jnp.abs(qpos - kpos) <= A_WINDOW
    s_loc = jnp.where(ok, _dot_nt(q, k_loc), NEG_INF)
    s_ctx = _dot_nt(q, k_ctx)
    sink = sink_ref[...]
    m = jnp.maximum(jnp.maximum(jnp.max(s_loc, axis=-1, keepdims=True), jnp.max(s_ctx, axis=-1, keepdims=True)), sink)
    p_loc = jnp.exp(s_loc - m)
    p_ctx = jnp.exp(s_ctx - m)
    denom = jnp.sum(p_loc, axis=-1, keepdims=True) + jnp.sum(p_ctx, axis=-1, keepdims=True) + jnp.exp(sink - m)
    o = (_dot(p_loc.astype(MXU_DTYPE), v_loc) + _dot(p_ctx.astype(MXU_DTYPE), v_ctx)) / denom
    o_ref[...] = o.reshape(o_ref.shape).astype(o_ref.dtype)


def _window_attention(q, k, v, sink_col, n_lat, n_ctx):
    b, hq, s, d = q.shape
    g = k.shape[1]
    r = hq // g
    tq = WIN_Q
    q5 = q.reshape(b, g, r, s, d)
    o = pl.pallas_call(
        functools.partial(_window_kernel, r=r, tq=tq, n_lat=n_lat, n_ctx=n_ctx),
        grid=(b, g, n_lat // tq),
        in_specs=[
            pl.BlockSpec((None, None, r, tq, d), lambda bi, gi, i: (bi, gi, 0, i, 0)),
            pl.BlockSpec((None, None, s, d), lambda bi, gi, i: (bi, gi, 0, 0)),
            pl.BlockSpec((None, None, s, d), lambda bi, gi, i: (bi, gi, 0, 0)),
            pl.BlockSpec((None, r * tq, 1), lambda bi, gi, i: (gi, 0, 0)),
            pl.BlockSpec(memory_space=pl.ANY),
        ],
        out_specs=pl.BlockSpec((None, None, r, tq, d), lambda bi, gi, i: (bi, gi, 0, i, 0)),
        out_shape=jax.ShapeDtypeStruct((b, g, r, s, d), MXU_DTYPE),
        input_output_aliases={4: 0},
        compiler_params=_cparams(("arbitrary", "arbitrary", "arbitrary")),
        name="window_attention",
    )(q5, k, v, sink_col, jnp.zeros((b, g, r, s, d), MXU_DTYPE))
    return o.reshape(b, hq, s, d)


def _toeplitz(a, n_q, n_k, off):
    length = a.shape[-1]
    lo = (n_q - 1) - off
    hi = (n_q + n_k - 1) - lo - length
    ap = jnp.pad(a, [(0, 0)] * (a.ndim - 1) + [(max(lo, 0), max(hi, 0))])
    ap = ap[..., max(-lo, 0):ap.shape[-1] - max(-hi, 0)]
    return jnp.stack([ap[..., n_q - 1 - q:n_q - 1 - q + n_k] for q in range(n_q)], axis=-2)


def _na_tables(rpb, rows):
    w = GRID_W
    kh = min(NA_ROWS, rows)
    configs = [(0, 0), (NA_QROWS, NA_QROWS - NA_ROWS // 2), (rows - NA_QROWS, rows - NA_KROWS)]
    qi = jnp.arange(NA_QROWS, dtype=jnp.int32)[:, None, None, None]
    qc = jnp.arange(w, dtype=jnp.int32)[None, :, None, None]
    kj = jnp.arange(NA_KROWS, dtype=jnp.int32)[None, None, :, None]
    kc = jnp.arange(w, dtype=jnp.int32)[None, None, None, :]
    full = (NA_QROWS, w, NA_KROWS, w)
    flat = (NA_QROWS * w, NA_KROWS * w)
    by_col = jnp.moveaxis(_toeplitz(rpb.astype(jnp.float32), w, w, NA_COLS - 1), 1, -1)
    tabs = []
    for r0, kr0 in configs:
        qr = r0 + qi
        kr = kr0 + kj
        rs = jnp.clip(qr - kh // 2, 0, rows - kh)
        cs = jnp.clip(qc - NA_COLS // 2, 0, w - NA_COLS)
        valid = (kr >= rs) & (kr < rs + kh) & (kc >= cs) & (kc < cs + NA_COLS)
        valid = jnp.broadcast_to(valid, full).reshape(flat)
        bias = _toeplitz(by_col, NA_QROWS, NA_KROWS, kr0 - r0 + NA_ROWS - 1)
        bias = jnp.transpose(bias, (0, 3, 1, 4, 2)).reshape((-1,) + flat)
        tabs.append(jnp.where(valid[None], bias, NEG_INF))
    return jnp.stack(tabs, axis=1)


def _na_kernel(q_ref, k_ref, v_ref, tab_ref, init_ref, o_ref, *, tq, tkw, n_lat, n_ctx, nq):
    del init_ref
    i = pl.program_id(2)
    kind = jnp.where(i == 0, 0, jnp.where(i == nq - 1, 2, 1))
    start = pl.multiple_of(jnp.clip(i * tq - (NA_ROWS // 2) * GRID_W, 0, n_lat - tkw), GRID_W)
    q = q_ref[...]
    k_loc = k_ref[pl.ds(start, tkw), :]
    v_loc = v_ref[pl.ds(start, tkw), :]
    k_ctx = k_ref[n_lat:n_lat + n_ctx, :]
    v_ctx = v_ref[n_lat:n_lat + n_ctx, :]
    s_loc = _dot_nt(q, k_loc) + tab_ref[kind]
    s_ctx = _dot_nt(q, k_ctx)
    m = jnp.maximum(jnp.max(s_loc, axis=-1, keepdims=True), jnp.max(s_ctx, axis=-1, keepdims=True))
    p_loc = jnp.exp(s_loc - m)
    p_ctx = jnp.exp(s_ctx - m)
    denom = jnp.sum(p_loc, axis=-1, keepdims=True) + jnp.sum(p_ctx, axis=-1, keepdims=True)
    o = (_dot(p_loc.astype(MXU_DTYPE), v_loc) + _dot(p_ctx.astype(MXU_DTYPE), v_ctx)) / denom
    o_ref[...] = o.astype(o_ref.dtype)


def _neighbourhood_attention(q, k, v, tables, n_lat, n_ctx):
    b, hq, s, d = q.shape
    tq = NA_QROWS * GRID_W
    tkw = NA_KROWS * GRID_W
    nq = n_lat // tq
    return pl.pallas_call(
        functools.partial(_na_kernel, tq=tq, tkw=tkw, n_lat=n_lat, n_ctx=n_ctx, nq=nq),
        grid=(b, hq, nq),
        in_specs=[
            pl.BlockSpec((None, None, tq, d), lambda bi, h, i: (bi, h, i, 0)),
            pl.BlockSpec((None, None, s, d), lambda bi, h, i: (bi, h, 0, 0)),
            pl.BlockSpec((None, None, s, d), lambda bi, h, i: (bi, h, 0, 0)),
            pl.BlockSpec((None, 3, tq, tkw), lambda bi, h, i: (h, 0, 0, 0)),
            pl.BlockSpec(memory_space=pl.ANY),
        ],
        out_specs=pl.BlockSpec((None, None, tq, d), lambda bi, h, i: (bi, h, i, 0)),
        out_shape=jax.ShapeDtypeStruct((b, hq, s, d), MXU_DTYPE),
        input_output_aliases={4: 0},
        compiler_params=_cparams(("arbitrary", "arbitrary", "arbitrary")),
        name="neighbourhood_attention",
    )(q, k, v, tables, jnp.zeros((b, hq, s, d), MXU_DTYPE))


def _outproj_kernel(o1_ref, o2_ref, w_ref, x_ref, mod_ref, rw_ref, rb_ref,
                    xo_ref, h_ref, lg_ref, *, tm, n_lat, n_heads, transposed):
    row0 = pl.program_id(1) * tm
    if transposed:
        yt = jnp.concatenate([o1_ref[...].reshape(-1, tm), o2_ref[...].reshape(-1, tm)], axis=0)
        z = lax.dot_general(yt, w_ref[...], (((0,), (0,)), ((), ())), preferred_element_type=jnp.float32)
    else:
        y = jnp.concatenate([o1_ref[h] for h in range(n_heads)] + [o2_ref[h] for h in range(n_heads)], axis=-1)
        z = _dot(y, w_ref[...])
    rows = row0 + lax.broadcasted_iota(jnp.int32, (tm, 1), 0)
    gate = jnp.where(rows >= n_lat, mod_ref[6:7, :], mod_ref[4:5, :])
    xn = x_ref[...] + gate * z
    xo_ref[...] = xn
    h = _modulated_norm(xn, mod_ref, row0, n_lat, 0, 1, 2, 3)
    _store_row_tiles(h_ref, h)
    lg_ref[...] = jnp.dot(h, rw_ref[...], preferred_element_type=jnp.float32,
                          precision=lax.Precision.HIGHEST) + rb_ref[...]


def _outproj(o1, o2, w_out, xs, modv, router_w, router_b, n_lat, transposed):
    b, s, dm = xs.shape
    tm = ROW_TILE
    nh = o1.shape[1]
    nl = router_w.shape[1]
    if transposed:
        o_spec = pl.BlockSpec((None, nh, o1.shape[2], tm), lambda bi, j: (bi, 0, 0, j))
    else:
        o_spec = pl.BlockSpec((None, nh, tm, o1.shape[3]), lambda bi, j: (bi, 0, j, 0))
    return pl.pallas_call(
        functools.partial(_outproj_kernel, tm=tm, n_lat=n_lat, n_heads=nh, transposed=transposed),
        grid=(b, s // tm),
        in_specs=[
            o_spec,
            o_spec,
            pl.BlockSpec(w_out.shape, lambda bi, j: (0, 0)),
            pl.BlockSpec((None, tm, dm), lambda bi, j: (bi, j, 0)),
            pl.BlockSpec((None, 8, dm), lambda bi, j: (bi, 0, 0)),
            pl.BlockSpec(router_w.shape, lambda bi, j: (0, 0)),
            pl.BlockSpec(router_b.shape, lambda bi, j: (0, 0)),
        ],
        out_specs=[pl.BlockSpec((None, tm, dm), lambda bi, j: (bi, j, 0)),
                   pl.BlockSpec((None, tm * ROW_TILES, LANES), lambda bi, j: (bi, j, 0)),
                   pl.BlockSpec((None, tm, nl), lambda bi, j: (bi, j, 0))],
        out_shape=[jax.ShapeDtypeStruct((b, s, dm), jnp.float32),
                   jax.ShapeDtypeStruct((b, s * ROW_TILES, LANES), jnp.float32),
                   jax.ShapeDtypeStruct((b, s, nl), jnp.float32)],
        input_output_aliases={3: 0},
        compiler_params=_cparams(("arbitrary", "arbitrary")),
        name="outproj_residual_norm_router",
    )(o1, o2, w_out, xs, modv, router_w, router_b)


def _expert_kernel(be_ref, tok0_ref, tokn_ref, h_hbm, win_ref, bin_ref, wout_ref, bout_ref,
                   y_ref, xbuf, sems, wi_sc, wo_sc, *, blk, nb):
    i = pl.program_id(0)
    slot = lax.rem(i, 2)
    rt = ROW_TILES

    @pl.when(jnp.logical_or(i == 0, be_ref[i] != be_ref[jnp.maximum(i - 1, 0)]))
    def _():
        wi_sc[...] = win_ref[...].astype(wi_sc.dtype)
        wo_sc[...] = wout_ref[...].astype(wo_sc.dtype)

    def gather(tok_ref, dst_slot):
        for rr in range(blk):
            src = pl.multiple_of(tok_ref[0, rr], rt)
            pltpu.make_async_copy(h_hbm.at[pl.ds(src, rt)], xbuf.at[dst_slot, pl.ds(rr * rt, rt)],
                                  sems.at[dst_slot]).start()

    @pl.when(i == 0)
    def _():
        gather(tok0_ref, 0)

    @pl.when(i + 1 < nb)
    def _():
        gather(tokn_ref, 1 - slot)

    pltpu.make_async_copy(h_hbm.at[pl.ds(0, blk * rt)], xbuf.at[slot], sems.at[slot]).wait()

    x = _load_row_tiles(xbuf.at[slot], blk).astype(MXU_DTYPE)
    u = _dot(x, wi_sc[...]) + bin_ref[...]
    glu = jnp.minimum(u[:, :D_EXPERT], SWIGLU_LIMIT)
    lin = jnp.clip(u[:, D_EXPERT:], -SWIGLU_LIMIT, SWIGLU_LIMIT)
    act = glu * jax.nn.sigmoid(SWIGLU_ALPHA * glu) * (lin + 1.0)
    _store_row_tiles(y_ref, _dot(act.astype(MXU_DTYPE), wo_sc[...]) + bout_ref[...])


def _experts(h_tiles, blk_expert, tok_buf, w_in, b_in, w_out, b_out, layer):
    dm = w_in.shape[2]
    blk = MOE_ROWS
    rt = ROW_TILES
    nb = blk_expert.shape[0]
    de2 = w_in.shape[-1]
    tok3 = tok_buf.reshape(nb, 1, blk)
    grid_spec = pltpu.PrefetchScalarGridSpec(
        num_scalar_prefetch=1,
        grid=(nb,),
        in_specs=[
            pl.BlockSpec((None, 1, blk), lambda i, be: (0, 0, 0), memory_space=pltpu.SMEM),
            pl.BlockSpec((None, 1, blk), lambda i, be: (jnp.minimum(i + 1, nb - 1), 0, 0),
                         memory_space=pltpu.SMEM),
            pl.BlockSpec(memory_space=pl.ANY),
            pl.BlockSpec((None, None, dm, de2), lambda i, be: (layer, be[i], 0, 0)),
            pl.BlockSpec((None, 1, de2), lambda i, be: (be[i], 0, 0)),
            pl.BlockSpec((None, None, de2 // 2, dm), lambda i, be: (layer, be[i], 0, 0)),
            pl.BlockSpec((None, 1, dm), lambda i, be: (be[i], 0, 0)),
        ],
        out_specs=pl.BlockSpec((blk * rt, LANES), lambda i, be: (i, 0)),
        scratch_shapes=[pltpu.VMEM((2, blk * rt, LANES), jnp.float32), pltpu.SemaphoreType.DMA((2,)),
                        pltpu.VMEM((dm, de2), MXU_DTYPE), pltpu.VMEM((de2 // 2, dm), MXU_DTYPE)],
    )
    return pl.pallas_call(
        functools.partial(_expert_kernel, blk=blk, nb=nb),
        grid_spec=grid_spec,
        out_shape=jax.ShapeDtypeStruct((nb * blk * rt, LANES), jnp.float32),
        compiler_params=_cparams(("arbitrary",)),
        name="routed_experts",
    )(blk_expert, tok3, tok3, h_tiles, w_in, b_in.reshape(N_EXPERTS, 1, de2), w_out,
      b_out.reshape(N_EXPERTS, 1, dm))


def _combine_kernel(pos0_ref, posn_ref, y_hbm, x_ref, mod_ref, w_ref, o_ref, gbuf, sems, *, tc, nsteps, n_lat,
                    steps_per_batch):
    i = pl.program_id(0)
    slot = lax.rem(i, 2)

    rt = ROW_TILES

    def gather(pos_ref, dst_slot):
        def issue(t, carry):
            for kk in range(TOP_K):
                src = pl.multiple_of(pos_ref[0, t * TOP_K + kk], rt)
                dst = pl.multiple_of(t * rt, rt)
                pltpu.make_async_copy(y_hbm.at[pl.ds(src, rt)], gbuf.at[dst_slot, kk, pl.ds(dst, rt)],
                                      sems.at[dst_slot]).start()
            return carry
        lax.fori_loop(0, tc, issue, 0, unroll=8)

    @pl.when(i == 0)
    def _():
        gather(pos0_ref, 0)

    @pl.when(i + 1 < nsteps)
    def _():
        gather(posn_ref, 1 - slot)

    for kk in range(TOP_K):
        pltpu.make_async_copy(y_hbm.at[pl.ds(0, tc * rt)], gbuf.at[slot, kk], sems.at[slot]).wait()

    w = w_ref[...]
    y = _load_row_tiles(gbuf.at[slot, 0], tc) * w[:, 0:1]
    for kk in range(1, TOP_K):
        y = y + _load_row_tiles(gbuf.at[slot, kk], tc) * w[:, kk:kk + 1]
    row0 = lax.rem(i, steps_per_batch) * tc
    rows = row0 + lax.broadcasted_iota(jnp.int32, (tc, 1), 0)
    gate = jnp.where(rows >= n_lat, mod_ref[7:8, :], mod_ref[5:6, :])
    o_ref[...] = x_ref[...] + gate * y


def _combine(y_sorted, pos, weights, xs_flat, modv, s, n_lat):
    n, dm = xs_flat.shape
    tc = COMBINE_ROWS
    nsteps = n // tc
    spb = s // tc
    pos3 = pos.reshape(nsteps, 1, tc * TOP_K)
    return pl.pallas_call(
        functools.partial(_combine_kernel, tc=tc, nsteps=nsteps, n_lat=n_lat, steps_per_batch=spb),
        grid=(nsteps,),
        in_specs=[
            pl.BlockSpec((None, 1, tc * TOP_K), lambda i: (0, 0, 0), memory_space=pltpu.SMEM),
            pl.BlockSpec((None, 1, tc * TOP_K), lambda i: (jnp.minimum(i + 1, nsteps - 1), 0, 0),
                         memory_space=pltpu.SMEM),
            pl.BlockSpec(memory_space=pl.ANY),
            pl.BlockSpec((tc, dm), lambda i: (i, 0)),
            pl.BlockSpec((None, 8, dm), lambda i: (i // spb, 0, 0)),
            pl.BlockSpec((tc, TOP_K), lambda i: (i, 0)),
        ],
        out_specs=pl.BlockSpec((tc, dm), lambda i: (i, 0)),
        out_shape=jax.ShapeDtypeStruct((n, dm), jnp.float32),
        scratch_shapes=[pltpu.VMEM((2, TOP_K, tc * ROW_TILES, LANES), jnp.float32), pltpu.SemaphoreType.DMA((2,))],
        input_output_aliases={3: 0},
        compiler_params=_cparams(("arbitrary",)),
        name="expert_combine",
    )(pos3, pos3, y_sorted, xs_flat, modv, weights)


def _final_norm_kernel(x_ref, g_ref, o_ref):
    x = x_ref[...]
    o_ref[...] = x * lax.rsqrt(jnp.mean(x * x, axis=-1, keepdims=True) + NORM_EPS) * g_ref[...]


def _final_norm(xs, g, n_lat):
    b, s, dm = xs.shape
    tm = 512
    return pl.pallas_call(
        _final_norm_kernel,
        grid=(b, n_lat // tm),
        in_specs=[pl.BlockSpec((None, tm, dm), lambda bi, j: (bi, j, 0)),
                  pl.BlockSpec((1, dm), lambda bi, j: (0, 0))],
        out_specs=pl.BlockSpec((None, tm, dm), lambda bi, j: (bi, j, 0)),
        out_shape=jax.ShapeDtypeStruct((b, n_lat, dm), jnp.float32),
        compiler_params=_cparams(("arbitrary", "arbitrary")),
        name="final_norm",
    )(xs, g.reshape(1, dm))


def _swap_perm(width, dim):
    j = np.arange(width)
    half = dim // 2
    return np.where((j % dim) < half, j + half, j - half)


def _rope_tables(n_lat, n_ctx, dim):
    t = jnp.arange(n_lat, dtype=jnp.int32)
    row = (t // GRID_W).astype(jnp.float32)
    col = (t % GRID_W).astype(jnp.float32)
    quarter = dim // 4
    inv_freq = ROPE_BASE ** (-jnp.arange(quarter, dtype=jnp.float32) / quarter)
    ang = jnp.concatenate([row[:, None] * inv_freq, col[:, None] * inv_freq], axis=-1)
    cos, sin = jnp.cos(ang), jnp.sin(ang)
    reps = LANES // dim
    cos_t = jnp.tile(jnp.concatenate([cos, cos], axis=-1), (1, reps))
    sin_t = jnp.tile(jnp.concatenate([-sin, sin], axis=-1), (1, reps))
    cos_t = jnp.concatenate([cos_t, jnp.ones((n_ctx, LANES), jnp.float32)], axis=0)
    sin_t = jnp.concatenate([sin_t, jnp.zeros((n_ctx, LANES), jnp.float32)], axis=0)
    return cos_t, sin_t


def _mod_rows(mod_l, norm_g, b, first):
    lat, ctx = mod_l[:b], mod_l[b]
    a_lat = norm_g[None] * (1.0 + lat[:, first + 1])
    b_lat = lat[:, first]
    a_ctx = jnp.broadcast_to(norm_g * (1.0 + ctx[first + 1]), a_lat.shape)
    b_ctx = jnp.broadcast_to(ctx[first], a_lat.shape)
    g_mix_ctx = jnp.broadcast_to(ctx[2], a_lat.shape)
    g_ffn_ctx = jnp.broadcast_to(ctx[5], a_lat.shape)
    return jnp.stack([a_lat, b_lat, a_ctx, b_ctx, lat[:, 2], lat[:, 5], g_mix_ctx, g_ffn_ctx], axis=1)


def _route(logits, blk):
    n = logits.shape[0]
    nk = n * TOP_K
    top_val, top_idx = lax.top_k(logits, TOP_K)
    gate = jax.nn.softmax(top_val, axis=-1)
    expert = top_idx.reshape(-1)
    onehot = (expert[:, None] == jnp.arange(N_EXPERTS, dtype=jnp.int32)[None, :]).astype(jnp.int32)
    csum = jnp.cumsum(onehot, axis=0)
    rank = jnp.take_along_axis(csum, expert[:, None], axis=1)[:, 0] - 1
    sizes = csum[-1]
    padded = (sizes + blk - 1) // blk * blk
    pends = jnp.cumsum(padded)
    pstarts = pends - padded
    dest = pstarts[expert] + rank
    nb = -(-(nk + N_EXPERTS * (blk - 1)) // blk)
    blk_expert = jnp.minimum(jnp.searchsorted(pends, jnp.arange(nb, dtype=jnp.int32) * blk, side='right'),
                             N_EXPERTS - 1).astype(jnp.int32)
    order = jnp.argsort(expert)
    starts = jnp.cumsum(sizes) - sizes
    slot = jnp.arange(nb * blk, dtype=jnp.int32)
    slot_expert = jnp.repeat(blk_expert, blk)
    slot_rank = slot - pstarts[slot_expert]
    src = jnp.minimum(starts[slot_expert] + slot_rank, nk - 1)
    tok_buf = jnp.where(slot_rank < sizes[slot_expert], order[src] // TOP_K, 0).astype(jnp.int32)
    return blk_expert, tok_buf, gate, dest.astype(jnp.int32)


def kernel(x, c, ctx, c_ctx, mod_w, mod_b, norm_mix, norm_ffn, ab_w_in, ab_w_out, a_sink, b_rpb,
           cd_w_in, c_q_norm, c_w_q_b, c_kv_norm, c_w_kv_b, d_q_norm, d_k_norm, cd_w_out,
           router_w, router_b, exp_w_in, exp_b_in, exp_w_out, exp_b_out, final_norm):
    b, t, dm = x.shape
    assert dm == ROW_TILES * LANES
    n_ctx = ctx.shape[1]
    s = t + n_ctx
    depth = mod_w.shape[0]
    rows = t // GRID_W
    d = HEAD_DIM
    f32 = jnp.float32

    xs = jnp.concatenate([x, ctx], axis=1)

    cond = jnp.zeros((8, dm), f32).at[:b].set(c).at[b].set(c_ctx)
    mod_all = _modulation(cond, mod_w, mod_b)[:, :b + 1].reshape(depth, b + 1, 6, dm)

    cos_t, sin_t = _rope_tables(t, n_ctx, d)
    cosm_t, sinm_t = _rope_tables(t, n_ctx, C_ROPE)

    router_w_p = jnp.zeros((depth, dm, LANES), f32).at[:, :, :N_EXPERTS].set(router_w)
    router_b_p = jnp.zeros((depth, 1, LANES), f32).at[:, 0, :N_EXPERTS].set(router_b)

    p64_512 = _swap_perm(512, d)
    p64_128 = _swap_perm(128, d)
    p32 = _swap_perm(C_ROPE, C_ROPE)

    for layer in range(depth):
        i = layer // 2
        modv_mix = _mod_rows(mod_all[layer], norm_mix[layer], b, 0)
        modv_ffn = _mod_rows(mod_all[layer], norm_ffn[layer], b, 3)
        if layer % 2 == 0:
            w = ab_w_in[i]
            w_ext = jnp.concatenate([w, w[:, 0:512][:, p64_512], w[:, 512:640][:, p64_128]], axis=1).astype(MXU_DTYPE)
            qa, ka, va, qb, kb, vb = _proj_ab(xs, modv_mix, w_ext, cos_t, sin_t, t)
            sink_gr = a_sink[i].astype(f32).reshape(A_KV_HEADS, A_HEADS // A_KV_HEADS)
            r = A_HEADS // A_KV_HEADS
            sink_win = jnp.repeat(sink_gr, WIN_Q, axis=1).reshape(A_KV_HEADS, r * WIN_Q, 1)
            sink_ctx = jnp.repeat(sink_gr, n_ctx, axis=1).reshape(A_KV_HEADS, r * n_ctx, 1)
            o1 = _window_attention(qa, ka, va, sink_win, t, n_ctx)
            o1 = _flash(qa, ka, va, groups=A_KV_HEADS, tq=n_ctx, tk=n_ctx, q_row0=t, n_q=n_ctx,
                        kv_row0=t, n_kv=n_ctx, sink=sink_ctx, out=o1)
            tables = _na_tables(b_rpb[i], rows)
            o2 = _neighbourhood_attention(qb, kb, vb, tables, t, n_ctx)
            o2 = _flash(qb, kb, vb, groups=B_HEADS, tq=n_ctx, tk=n_ctx, q_row0=t, n_q=n_ctx,
                        kv_row0=t, n_kv=n_ctx, out=o2)
            w_out = ab_w_out[i].astype(MXU_DTYPE)
        else:
            w = cd_w_in[i]
            base = C_Q_RANK + C_KV_RANK
            o_qd, o_kd, o_vd = base + C_ROPE, base + C_ROPE + 512, base + C_ROPE + 640
            wqd, wkd = w[:, o_qd:o_qd + 512], w[:, o_kd:o_kd + 128]
            gq = jnp.tile(d_q_norm[i].astype(f32), D_HEADS)
            gk = jnp.tile(d_k_norm[i].astype(f32), D_KV_HEADS)
            krope = w[:, base:base + C_ROPE]
            pad = jnp.zeros((dm, LANES - C_ROPE), f32)
            w1 = jnp.concatenate([
                w[:, 0:base], wqd, wkd, w[:, o_vd:o_vd + 128],
                (wqd * gq[None])[:, p64_512], (wkd * gk[None])[:, p64_128],
                krope, pad, krope[:, p32], pad], axis=1).astype(MXU_DTYPE)
            dq = C_NOPE + C_ROPE
            wq = c_w_q_b[i].reshape(C_Q_RANK, C_HEADS, dq)
            wq_nope = wq[:, :, :C_NOPE].reshape(C_Q_RANK, C_HEADS * C_NOPE)
            wq_rope = wq[:, :, C_NOPE:].reshape(C_Q_RANK, C_HEADS * C_ROPE)
            wq2 = jnp.concatenate([wq_nope, wq_rope, wq_rope[:, _swap_perm(C_HEADS * C_ROPE, C_ROPE)]],
                                  axis=1).astype(MXU_DTYPE)
            wkv = c_w_kv_b[i].reshape(C_KV_RANK, C_HEADS, C_NOPE + C_V)
            wkv2 = jnp.concatenate([wkv[:, :, :C_NOPE].reshape(C_KV_RANK, -1),
                                    wkv[:, :, C_NOPE:].reshape(C_KV_RANK, -1)], axis=1).astype(MXU_DTYPE)
            qc, kc, vc, qd, kd, vd = _proj_cd(
                xs, modv_mix, w1, wq2, wkv2, c_q_norm[i].astype(f32).reshape(1, -1),
                c_kv_norm[i].astype(f32).reshape(1, -1), gq.reshape(1, -1), gk.reshape(1, -1),
                cos_t, sin_t, cosm_t, sinm_t, t)
            o1 = _flash_t(qc, kc, vc, groups=C_HEADS, tq=FLASH_M, tk=FLASH_TK, q_row0=0, n_q=t, kv_row0=0, n_kv=s,
                          out=jnp.zeros((b, C_HEADS, C_V, s), MXU_DTYPE))
            o1 = _flash_t(qc, kc, vc, groups=C_HEADS, tq=n_ctx, tk=n_ctx, q_row0=t, n_q=n_ctx,
                          kv_row0=t, n_kv=n_ctx, out=o1)
            rd = D_HEADS // D_KV_HEADS
            o2 = _flash_t(qd, kd, vd, groups=D_KV_HEADS, tq=FLASH_M // rd, tk=FLASH_TK, q_row0=0, n_q=t,
                          kv_row0=0, n_kv=s, out=jnp.zeros((b, D_HEADS, d, s), MXU_DTYPE))
            o2 = _flash_t(qd, kd, vd, groups=D_KV_HEADS, tq=n_ctx, tk=n_ctx, q_row0=t, n_q=n_ctx,
                          kv_row0=t, n_kv=n_ctx, out=o2)
            w_out = cd_w_out[i].astype(MXU_DTYPE)

        xs, h2, logits = _outproj(o1, o2, w_out, xs, modv_ffn, router_w_p[layer], router_b_p[layer], t,
                                  transposed=layer % 2 == 1)

        n = b * s
        blk_expert, tok_buf, weights, pos = _route(logits.reshape(n, LANES)[:, :N_EXPERTS], MOE_ROWS)
        y_sorted = _experts(h2.reshape(n * ROW_TILES, LANES), blk_expert, tok_buf * ROW_TILES,
                            exp_w_in.astype(f32), exp_b_in[layer].astype(f32),
                            exp_w_out.astype(f32), exp_b_out[layer].astype(f32), layer)
        xs = _combine(y_sorted, pos * ROW_TILES, weights, xs.reshape(n, dm), modv_ffn, s, t).reshape(b, s, dm)

    return _final_norm(xs, final_norm.astype(f32), t)
```

```python
import functools

import jax
import jax.numpy as jnp
import numpy as np
from jax import lax
from jax.experimental import pallas as pl
from jax.experimental.pallas import tpu as pltpu

GRID_W = 64
HEAD_DIM = 64
ROPE_BASE = 10000.0
NORM_EPS = 1e-6
NEG_INF = -1e30
A_HEADS = 8
A_KV_HEADS = 2
A_WINDOW = 128
B_HEADS = 8
NA_ROWS = 8
NA_COLS = 16
C_HEADS = 8
C_Q_RANK = 768
C_KV_RANK = 256
C_NOPE = 64
C_ROPE = 32
C_V = 64
D_HEADS = 8
D_KV_HEADS = 2
N_EXPERTS = 32
TOP_K = 4
D_EXPERT = 1024
SWIGLU_LIMIT = 7.0
SWIGLU_ALPHA = 1.702
LOG2_E = 1.4426950408889634

LANES = 128
BF16_SUBLANES = 16
ROW_TILES = 8
VMEM_LIMIT_BYTES = 56 * 1024 * 1024

MXU_DTYPE = jnp.bfloat16

ROW_TILE = 640
MOE_ROWS = 256
COMBINE_ROWS = 256
NA_QROWS = 8
NA_KROWS = 16
WIN_Q = 256
ATTN_SPLIT = 2
FLASH_M = 1024
FLASH_TK = 640


def _cparams(sem, vmem=VMEM_LIMIT_BYTES):
    return pltpu.CompilerParams(dimension_semantics=sem, vmem_limit_bytes=vmem)


def _dot(a, b):
    return jnp.dot(a, b, preferred_element_type=jnp.float32)


def _dot_nt(a, b):
    return lax.dot_general(a, b, (((1,), (1,)), ((), ())), preferred_element_type=jnp.float32)


def _mod_kernel(c_ref, w_ref, b_ref, o_ref):
    c = c_ref[...]
    s = c * jax.nn.sigmoid(c)
    o_ref[...] = jnp.dot(s, w_ref[...], preferred_element_type=jnp.float32,
                         precision=lax.Precision.HIGHEST) + b_ref[...]


def _modulation(cond, mod_w, mod_b):
    depth, dm, n = mod_w.shape
    tn = 1536
    return pl.pallas_call(
        _mod_kernel,
        grid=(depth, n // tn),
        in_specs=[
            pl.BlockSpec((8, dm), lambda l, j: (0, 0)),
            pl.BlockSpec((None, dm, tn), lambda l, j: (l, 0, j)),
            pl.BlockSpec((None, 1, tn), lambda l, j: (l, 0, j)),
        ],
        out_specs=pl.BlockSpec((None, 8, tn), lambda l, j: (l, 0, j)),
        out_shape=jax.ShapeDtypeStruct((depth, 8, n), jnp.float32),
        compiler_params=_cparams(("arbitrary", "arbitrary")),
        name="adaln_modulation",
    )(cond, mod_w, mod_b.reshape(depth, 1, n))


def _modulated_norm(x, mod_ref, row0, n_lat, a_lat, b_lat, a_ctx, b_ctx):
    tm = x.shape[0]
    ms = jnp.mean(x * x, axis=-1, keepdims=True)
    xh = x * lax.rsqrt(ms + NORM_EPS)
    rows = row0 + lax.broadcasted_iota(jnp.int32, (tm, 1), 0)
    is_ctx = rows >= n_lat
    a = jnp.where(is_ctx, mod_ref[a_ctx:a_ctx + 1, :], mod_ref[a_lat:a_lat + 1, :])
    b = jnp.where(is_ctx, mod_ref[b_ctx:b_ctx + 1, :], mod_ref[b_lat:b_lat + 1, :])
    return xh * a + b


def _store_heads(o_ref, val, n_heads, width, lane0=0):
    for h in range(n_heads):
        o_ref[h, :, lane0:lane0 + width] = val[:, h * width:(h + 1) * width].astype(o_ref.dtype)


def _store_row_tiles(o_ref, val):
    rows = val.shape[0]
    for c in range(ROW_TILES):
        o_ref[pl.ds(c, rows, stride=ROW_TILES), :] = val[:, c * LANES:(c + 1) * LANES]


def _load_row_tiles(ref, rows):
    return jnp.concatenate([ref[pl.ds(c, rows, stride=ROW_TILES), :] for c in range(ROW_TILES)], axis=-1)


def _tile_lanes(t, reps):
    return t if reps == 1 else jnp.concatenate([t] * reps, axis=-1)


def _proj_ab_kernel(x_ref, mod_ref, w_ref, cos_ref, sin_ref,
                    qa_ref, ka_ref, va_ref, qb_ref, kb_ref, vb_ref, *, tm, n_lat):
    row0 = pl.program_id(1) * tm
    h = _modulated_norm(x_ref[...], mod_ref, row0, n_lat, 0, 1, 2, 3).astype(MXU_DTYPE)
    cos = cos_ref[...]
    sin = sin_ref[...]
    d = HEAD_DIM
    scale = d ** -0.5
    qa = _dot(h, w_ref[:, 0:512]) * _tile_lanes(cos, 4) + _dot(h, w_ref[:, 2304:2816]) * _tile_lanes(sin, 4)
    _store_heads(qa_ref, qa * scale, A_HEADS, d)
    ka = _dot(h, w_ref[:, 512:640]) * cos + _dot(h, w_ref[:, 2816:2944]) * sin
    _store_heads(ka_ref, ka, A_KV_HEADS, d)
    _store_heads(va_ref, _dot(h, w_ref[:, 640:768]), A_KV_HEADS, d)
    _store_heads(qb_ref, _dot(h, w_ref[:, 768:1280]) * scale, B_HEADS, d)
    _store_heads(kb_ref, _dot(h, w_ref[:, 1280:1792]), B_HEADS, d)
    _store_heads(vb_ref, _dot(h, w_ref[:, 1792:2304]), B_HEADS, d)


def _proj_ab(xs, modv, w_ext, cos_t, sin_t, n_lat):
    b, s, dm = xs.shape
    tm = ROW_TILE
    d = HEAD_DIM
    nw = w_ext.shape[1]

    def head_spec(nh):
        return pl.BlockSpec((None, nh, tm, d), lambda bi, j: (bi, 0, j, 0))

    def head_shape(nh):
        return jax.ShapeDtypeStruct((b, nh, s, d), MXU_DTYPE)

    return pl.pallas_call(
        functools.partial(_proj_ab_kernel, tm=tm, n_lat=n_lat),
        grid=(b, s // tm),
        in_specs=[
            pl.BlockSpec((None, tm, dm), lambda bi, j: (bi, j, 0)),
            pl.BlockSpec((None, 8, dm), lambda bi, j: (bi, 0, 0)),
            pl.BlockSpec((dm, nw), lambda bi, j: (0, 0)),
            pl.BlockSpec((tm, LANES), lambda bi, j: (j, 0)),
            pl.BlockSpec((tm, LANES), lambda bi, j: (j, 0)),
        ],
        out_specs=[head_spec(A_HEADS), head_spec(A_KV_HEADS), head_spec(A_KV_HEADS),
                   head_spec(B_HEADS), head_spec(B_HEADS), head_spec(B_HEADS)],
        out_shape=[head_shape(A_HEADS), head_shape(A_KV_HEADS), head_shape(A_KV_HEADS),
                   head_shape(B_HEADS), head_shape(B_HEADS), head_shape(B_HEADS)],
        compiler_params=_cparams(("arbitrary", "arbitrary")),
        name="proj_ab",
    )(xs, modv, w_ext, cos_t, sin_t)


def _proj_cd_kernel(x_ref, mod_ref, w1_ref, wq2_ref, wkv2_ref, qn_ref, kvn_ref, gq_ref, gk_ref,
                    cos_ref, sin_ref, cosm_ref, sinm_ref,
                    qc_ref, kc_ref, vc_ref, qd_ref, kd_ref, vd_ref, *, tm, n_lat):
    row0 = pl.program_id(1) * tm
    h = _modulated_norm(x_ref[...], mod_ref, row0, n_lat, 0, 1, 2, 3).astype(MXU_DTYPE)
    cos = cos_ref[...]
    sin = sin_ref[...]
    cosm = cosm_ref[...]
    sinm = sinm_ref[...]
    d = HEAD_DIM

    cq = _dot(h, w1_ref[:, 0:768])
    ckv = _dot(h, w1_ref[:, 768:1024])
    qd_raw, qd_swp = _dot(h, w1_ref[:, 1024:1536]), _dot(h, w1_ref[:, 1792:2304])
    kd_raw, kd_swp = _dot(h, w1_ref[:, 1536:1664]), _dot(h, w1_ref[:, 2304:2432])
    vd = _dot(h, w1_ref[:, 1664:1792])
    kr_raw, kr_swp = _dot(h, w1_ref[:, 2432:2560]), _dot(h, w1_ref[:, 2560:2688])

    cqn = cq * lax.rsqrt(jnp.mean(cq * cq, axis=-1, keepdims=True) + NORM_EPS) * qn_ref[...]
    ckvn = ckv * lax.rsqrt(jnp.mean(ckv * ckv, axis=-1, keepdims=True) + NORM_EPS) * kvn_ref[...]
    q2 = _dot(cqn.astype(MXU_DTYPE), wq2_ref[...])
    kv2 = _dot(ckvn.astype(MXU_DTYPE), wkv2_ref[...])

    def normed_rope(raw, swapped_gained, gain, n_heads, o_ref, scale):
        reps = n_heads * d // LANES
        val = raw * (_tile_lanes(cos, reps) * gain) + swapped_gained * _tile_lanes(sin, reps)
        for hh in range(n_heads):
            sl = raw[:, hh * d:(hh + 1) * d]
            r = lax.rsqrt(jnp.mean(sl * sl, axis=-1, keepdims=True) + NORM_EPS)
            o_ref[hh, :, :] = (val[:, hh * d:(hh + 1) * d] * (r * scale)).astype(o_ref.dtype)

    normed_rope(qd_raw, qd_swp, gq_ref[...], D_HEADS, qd_ref, d ** -0.5 * LOG2_E)
    normed_rope(kd_raw, kd_swp, gk_ref[...], D_KV_HEADS, kd_ref, 1.0)
    _store_heads(vd_ref, vd, D_KV_HEADS, d)

    c_scale = (C_NOPE + C_ROPE) ** -0.5 * LOG2_E
    q_rope = (q2[:, 512:768] * _tile_lanes(cosm, 2) + q2[:, 768:1024] * _tile_lanes(sinm, 2)) * c_scale
    _store_heads(qc_ref, q2[:, 0:512] * c_scale, C_HEADS, C_NOPE)
    _store_heads(qc_ref, q_rope, C_HEADS, C_ROPE, lane0=C_NOPE)
    k_rope = (kr_raw * cosm + kr_swp * sinm)[:, 0:C_ROPE]
    _store_heads(kc_ref, kv2[:, 0:512], C_HEADS, C_NOPE)
    for hh in range(C_HEADS):
        kc_ref[hh, :, C_NOPE:C_NOPE + C_ROPE] = k_rope.astype(kc_ref.dtype)
    _store_heads(vc_ref, kv2[:, 512:1024], C_HEADS, C_V)


def _proj_cd(xs, modv, w1, wq2, wkv2, qn, kvn, gq, gk, cos_t, sin_t, cosm_t, sinm_t, n_lat):
    b, s, dm = xs.shape
    tm = ROW_TILE
    dk_c = C_NOPE + C_ROPE

    def full(a):
        return pl.BlockSpec(a.shape, lambda bi, j: (0,) * a.ndim)

    def head_spec(nh, w):
        return pl.BlockSpec((None, nh, tm, w), lambda bi, j: (bi, 0, j, 0))

    def head_shape(nh, w):
        return jax.ShapeDtypeStruct((b, nh, s, w), MXU_DTYPE)

    tab = pl.BlockSpec((tm, LANES), lambda bi, j: (j, 0))
    return pl.pallas_call(
        functools.partial(_proj_cd_kernel, tm=tm, n_lat=n_lat),
        grid=(b, s // tm),
        in_specs=[
            pl.BlockSpec((None, tm, dm), lambda bi, j: (bi, j, 0)),
            pl.BlockSpec((None, 8, dm), lambda bi, j: (bi, 0, 0)),
            full(w1), full(wq2), full(wkv2), full(qn), full(kvn), full(gq), full(gk),
            tab, tab, tab, tab,
        ],
        out_specs=[head_spec(C_HEADS, dk_c), head_spec(C_HEADS, dk_c), head_spec(C_HEADS, C_V),
                   head_spec(D_HEADS, HEAD_DIM), head_spec(D_KV_HEADS, HEAD_DIM), head_spec(D_KV_HEADS, HEAD_DIM)],
        out_shape=[head_shape(C_HEADS, dk_c), head_shape(C_HEADS, dk_c), head_shape(C_HEADS, C_V),
                   head_shape(D_HEADS, HEAD_DIM), head_shape(D_KV_HEADS, HEAD_DIM), head_shape(D_KV_HEADS, HEAD_DIM)],
        compiler_params=_cparams(("arbitrary", "arbitrary")),
        name="proj_cd",
    )(xs, modv, w1, wq2, wkv2, qn, kvn, gq, gk, cos_t, sin_t, cosm_t, sinm_t)


def _flash_kernel(*refs, r, tq, tk, nk, has_sink):
    if has_sink:
        q_ref, k_ref, v_ref, sink_ref, o_ref, m_sc, l_sc, acc_sc = refs
    else:
        q_ref, k_ref, v_ref, o_ref, m_sc, l_sc, acc_sc = refs
    m_rows = r * tq
    q = q_ref[...].reshape(m_rows, q_ref.shape[-1])
    if has_sink:
        m_sc[...] = sink_ref[...]
        l_sc[...] = jnp.ones_like(l_sc)
    else:
        m_sc[...] = jnp.full_like(m_sc, NEG_INF)
        l_sc[...] = jnp.zeros_like(l_sc)
    acc_sc[...] = jnp.zeros_like(acc_sc)

    def body(j, carry):
        k0 = pl.multiple_of(j * tk, tk)
        k = k_ref[pl.ds(k0, tk), :]
        v = v_ref[pl.ds(k0, tk), :]
        s = _dot_nt(q, k)
        m_prev = m_sc[...]
        m_new = jnp.maximum(m_prev, jnp.max(s, axis=-1, keepdims=True))
        alpha = jnp.exp(m_prev - m_new)
        p = jnp.exp(s - m_new)
        l_sc[...] = alpha * l_sc[...] + jnp.sum(p, axis=-1, keepdims=True)
        acc_sc[...] = alpha * acc_sc[...] + _dot(p.astype(MXU_DTYPE), v)
        m_sc[...] = m_new
        return carry

    lax.fori_loop(0, nk, body, 0)
    o = acc_sc[...] / l_sc[...]
    o_ref[...] = o.reshape(o_ref.shape).astype(o_ref.dtype)


def _flash(q, k, v, *, groups, tq, tk, q_row0, n_q, kv_row0, n_kv, sink=None, out=None):
    b, hq, s, dk = q.shape
    dv = v.shape[-1]
    r = hq // groups
    q5 = q.reshape(b, groups, r, s, dk)
    nk = n_kv // tk
    has_sink = sink is not None
    m_rows = r * tq
    in_specs = [
        pl.BlockSpec((None, None, r, tq, dk), lambda bi, g, i: (bi, g, 0, q_row0 // tq + i, 0)),
        pl.BlockSpec((None, None, n_kv, dk), lambda bi, g, i: (bi, g, kv_row0 // n_kv, 0)),
        pl.BlockSpec((None, None, n_kv, dv), lambda bi, g, i: (bi, g, kv_row0 // n_kv, 0)),
    ]
    args = [q5, k, v]
    if has_sink:
        in_specs.append(pl.BlockSpec((None, m_rows, 1), lambda bi, g, i: (g, 0, 0)))
        args.append(sink)
    aliases = {}
    if out is not None:
        in_specs.append(pl.BlockSpec(memory_space=pl.ANY))
        args.append(out.reshape(b, groups, r, s, dv))
        aliases = {len(args) - 1: 0}
    kern = functools.partial(_flash_kernel, r=r, tq=tq, tk=tk, nk=nk, has_sink=has_sink)
    if out is not None:
        inner = kern
        kern = lambda *refs: inner(*refs[:len(args) - 1], *refs[len(args):])
    o = pl.pallas_call(
        kern,
        grid=(b, groups, n_q // tq),
        in_specs=in_specs,
        out_specs=pl.BlockSpec((None, None, r, tq, dv), lambda bi, g, i: (bi, g, 0, q_row0 // tq + i, 0)),
        out_shape=jax.ShapeDtypeStruct((b, groups, r, s, dv), MXU_DTYPE),
        scratch_shapes=[pltpu.VMEM((m_rows, 1), jnp.float32), pltpu.VMEM((m_rows, 1), jnp.float32),
                        pltpu.VMEM((m_rows, dv), jnp.float32)],
        input_output_aliases=aliases,
        compiler_params=_cparams(("arbitrary", "arbitrary", "arbitrary")),
        name="dense_attention",
    )(*args)
    return o.reshape(b, hq, s, dv)


def _flash_t_kernel(q_ref, k_ref, vt_ref, init_ref, o_ref, m_sc, acc_sc, sa_sc, sb_sc, *, r, tq, tk, nk, dv):
    del init_ref
    m_cols = r * tq
    q = q_ref[...].reshape(m_cols, q_ref.shape[-1])
    m_sc[...] = jnp.full_like(m_sc, NEG_INF)
    acc_sc[...] = jnp.zeros_like(acc_sc)

    def scores(j, s_ref):
        k0 = pl.multiple_of(j * tk, tk)
        s_ref[...] = _dot_nt(k_ref[pl.ds(k0, tk), :], q)

    def update(j, s_ref):
        st = s_ref[...]
        m_prev = m_sc[...]
        m_new = jnp.maximum(m_prev, jnp.max(st, axis=0, keepdims=True))
        alpha = jnp.exp2(m_prev - m_new)
        pt = jnp.exp2(st - m_new)
        acc_sc[...] = alpha * acc_sc[...] + _dot(vt_ref[j], pt.astype(MXU_DTYPE))
        m_sc[...] = m_new

    scores(0, sa_sc)

    def body(jj, carry):
        j = 2 * jj
        scores(j + 1, sb_sc)
        update(j, sa_sc)
        scores(j + 2, sa_sc)
        update(j + 1, sb_sc)
        return carry

    lax.fori_loop(0, (nk - 1) // 2, body, 0)
    if nk % 2 == 0:
        scores(nk - 1, sb_sc)
        update(nk - 2, sa_sc)
        update(nk - 1, sb_sc)
    else:
        update(nk - 1, sa_sc)
    o = acc_sc[0:dv, :] / acc_sc[dv:dv + 1, :]
    for rr in range(r):
        o_ref[rr] = o[:, rr * tq:(rr + 1) * tq].astype(o_ref.dtype)


def _flash_t(q, k, v, *, groups, tq, tk, q_row0, n_q, kv_row0, n_kv, out):
    b, hq, s, dk = q.shape
    dv = v.shape[-1]
    r = hq // groups
    q5 = q.reshape(b, groups, r, s, dk)
    nk = n_kv // tk
    vt = jnp.swapaxes(lax.slice_in_dim(v, kv_row0, kv_row0 + n_kv, axis=2).reshape(b, groups, nk, tk, dv), 3, 4)
    dva = dv + BF16_SUBLANES
    vt = jnp.concatenate([vt, jnp.ones((b, groups, nk, BF16_SUBLANES, tk), vt.dtype)], axis=3)
    m_cols = r * tq
    o = pl.pallas_call(
        functools.partial(_flash_t_kernel, r=r, tq=tq, tk=tk, nk=nk, dv=dv),
        grid=(b, groups, n_q // tq),
        in_specs=[
            pl.BlockSpec((None, None, r, tq, dk), lambda bi, g, i: (bi, g, 0, q_row0 // tq + i, 0)),
            pl.BlockSpec((None, None, n_kv, dk), lambda bi, g, i: (bi, g, kv_row0 // n_kv, 0)),
            pl.BlockSpec((None, None, nk, dva, tk), lambda bi, g, i: (bi, g, 0, 0, 0)),
            pl.BlockSpec(memory_space=pl.ANY),
        ],
        out_specs=pl.BlockSpec((None, None, r, dv, tq), lambda bi, g, i: (bi, g, 0, 0, q_row0 // tq + i)),
        out_shape=jax.ShapeDtypeStruct((b, groups, r, dv, s), MXU_DTYPE),
        scratch_shapes=[pltpu.VMEM((1, m_cols), jnp.float32),
                        pltpu.VMEM((dva, m_cols), jnp.float32),
                        pltpu.VMEM((tk, m_cols), jnp.float32), pltpu.VMEM((tk, m_cols), jnp.float32)],
        input_output_aliases={3: 0},
        compiler_params=_cparams(("arbitrary", "arbitrary", "arbitrary")),
        name="dense_attention_t",
    )(q5, k, vt, out.reshape(b, groups, r, dv, s))
    return o.reshape(b, hq, dv, s)


def _window_kernel(q_ref, k_ref, v_ref, sink_ref, init_ref, o_ref, *, r, tq, n_lat, n_ctx):
    del init_ref
    i = pl.program_id(2)
    span = tq + 2 * A_WINDOW
    d = q_ref.shape[-1]
    start = pl.multiple_of(jnp.clip(i * tq - A_WINDOW, 0, n_lat - span), A_WINDOW)
    k_loc = k_ref[pl.ds(start, span), :]
    v_loc = v_ref[pl.ds(start, span), :]
    k_ctx = k_ref[n_lat:n_lat + n_ctx, :]
    v_ctx = v_ref[n_lat:n_lat + n_ctx, :]
    hp = r // ATTN_SPLIT
    rows = hp * tq
    qs = [q_ref[p * hp:(p + 1) * hp].reshape(rows, d) for p in range(ATTN_SPLIT)]
    scores = [(_dot_nt(qp, k_loc), _dot_nt(qp, k_ctx)) for qp in qs]
    qpos = i * tq + lax.rem(lax.broadcasted_iota(jnp.int32, (rows, span), 0), tq)
    kpos = start + lax.broadcasted_iota(jnp.int32, (rows, span), 1)
    ok = jnp.abs(qpos - kpos) <= A_WINDOW
    for p, (s_loc, s_ctx) in enumerate(scores):
        s_loc = jnp.where(ok, s_loc, NEG_INF)
        sink = sink_ref[p * rows:(p + 1) * rows, :]
        m = jnp.maximum(jnp.maximum(jnp.max(s_loc, axis=-1, keepdims=True),
                                    jnp.max(s_ctx, axis=-1, keepdims=True)), sink)
        p_loc = jnp.exp(s_loc - m)
        p_ctx = jnp.exp(s_ctx - m)
        denom = (jnp.sum(p_loc, axis=-1, keepdims=True) + jnp.sum(p_ctx, axis=-1, keepdims=True)
                 + jnp.exp(sink - m))
        o = (_dot(p_loc.astype(MXU_DTYPE), v_loc) + _dot(p_ctx.astype(MXU_DTYPE), v_ctx)) / denom
        o_ref[p * hp:(p + 1) * hp] = o.reshape(hp, tq, d).astype(o_ref.dtype)


def _window_attention(q, k, v, sink_col, n_lat, n_ctx):
    b, hq, s, d = q.shape
    g = k.shape[1]
    r = hq // g
    tq = WIN_Q
    q5 = q.reshape(b, g, r, s, d)
    o = pl.pallas_call(
        functools.partial(_window_kernel, r=r, tq=tq, n_lat=n_lat, n_ctx=n_ctx),
        grid=(b, g, n_lat // tq),
        in_specs=[
            pl.BlockSpec((None, None, r, tq, d), lambda bi, gi, i: (bi, gi, 0, i, 0)),
            pl.BlockSpec((None, None, s, d), lambda bi, gi, i: (bi, gi, 0, 0)),
            pl.BlockSpec((None, None, s, d), lambda bi, gi, i: (bi, gi, 0, 0)),
            pl.BlockSpec((None, r * tq, 1), lambda bi, gi, i: (gi, 0, 0)),
            pl.BlockSpec(memory_space=pl.ANY),
        ],
        out_specs=pl.BlockSpec((None, None, r, tq, d), lambda bi, gi, i: (bi, gi, 0, i, 0)),
        out_shape=jax.ShapeDtypeStruct((b, g, r, s, d), MXU_DTYPE),
        input_output_aliases={4: 0},
        compiler_params=_cparams(("arbitrary", "arbitrary", "arbitrary")),
        name="window_attention",
    )(q5, k, v, sink_col, jnp.zeros((b, g, r, s, d), MXU_DTYPE))
    return o.reshape(b, hq, s, d)


def _toeplitz(a, n_q, n_k, off):
    length = a.shape[-1]
    lo = (n_q - 1) - off
    hi = (n_q + n_k - 1) - lo - length
    ap = jnp.pad(a, [(0, 0)] * (a.ndim - 1) + [(max(lo, 0), max(hi, 0))])
    ap = ap[..., max(-lo, 0):ap.shape[-1] - max(-hi, 0)]
    return jnp.stack([ap[..., n_q - 1 - q:n_q - 1 - q + n_k] for q in range(n_q)], axis=-2)


def _na_tables(rpb, rows):
    w = GRID_W
    kh = min(NA_ROWS, rows)
    configs = [(0, 0), (NA_QROWS, NA_QROWS - NA_ROWS // 2), (rows - NA_QROWS, rows - NA_KROWS)]
    qi = jnp.arange(NA_QROWS, dtype=jnp.int32)[:, None, None, None]
    qc = jnp.arange(w, dtype=jnp.int32)[None, :, None, None]
    kj = jnp.arange(NA_KROWS, dtype=jnp.int32)[None, None, :, None]
    kc = jnp.arange(w, dtype=jnp.int32)[None, None, None, :]
    full = (NA_QROWS, w, NA_KROWS, w)
    flat = (NA_QROWS * w, NA_KROWS * w)
    by_col = jnp.moveaxis(_toeplitz(rpb.astype(jnp.float32), w, w, NA_COLS - 1), 1, -1)
    tabs = []
    for r0, kr0 in configs:
        qr = r0 + qi
        kr = kr0 + kj
        rs = jnp.clip(qr - kh // 2, 0, rows - kh)
        cs = jnp.clip(qc - NA_COLS // 2, 0, w - NA_COLS)
        valid = (kr >= rs) & (kr < rs + kh) & (kc >= cs) & (kc < cs + NA_COLS)
        valid = jnp.broadcast_to(valid, full).reshape(flat)
        bias = _toeplitz(by_col, NA_QROWS, NA_KROWS, kr0 - r0 + NA_ROWS - 1)
        bias = jnp.transpose(bias, (0, 3, 1, 4, 2)).reshape((-1,) + flat)
        tabs.append(jnp.where(valid[None], bias, NEG_INF))
    return jnp.stack(tabs, axis=1)


def _na_kernel(q_ref, k_ref, v_ref, tab_ref, init_ref, o_ref, *, tq, tkw, n_lat, n_ctx, nq):
    del init_ref
    i = pl.program_id(2)
    kind = jnp.where(i == 0, 0, jnp.where(i == nq - 1, 2, 1))
    start = pl.multiple_of(jnp.clip(i * tq - (NA_ROWS // 2) * GRID_W, 0, n_lat - tkw), GRID_W)
    k_loc = k_ref[pl.ds(start, tkw), :]
    v_loc = v_ref[pl.ds(start, tkw), :]
    k_ctx = k_ref[n_lat:n_lat + n_ctx, :]
    v_ctx = v_ref[n_lat:n_lat + n_ctx, :]
    parts = [(lo, lo + tq // ATTN_SPLIT) for lo in range(0, tq, tq // ATTN_SPLIT)]
    scores = [(_dot_nt(q_ref[lo:hi, :], k_loc), _dot_nt(q_ref[lo:hi, :], k_ctx)) for lo, hi in parts]
    for (lo, hi), (s_loc, s_ctx) in zip(parts, scores):
        s_loc = s_loc + tab_ref[kind, lo:hi, :]
        m = jnp.maximum(jnp.max(s_loc, axis=-1, keepdims=True), jnp.max(s_ctx, axis=-1, keepdims=True))
        p_loc = jnp.exp(s_loc - m)
        p_ctx = jnp.exp(s_ctx - m)
        denom = jnp.sum(p_loc, axis=-1, keepdims=True) + jnp.sum(p_ctx, axis=-1, keepdims=True)
        o = (_dot(p_loc.astype(MXU_DTYPE), v_loc) + _dot(p_ctx.astype(MXU_DTYPE), v_ctx)) / denom
        o_ref[lo:hi, :] = o.astype(o_ref.dtype)


def _neighbourhood_attention(q, k, v, tables, n_lat, n_ctx):
    b, hq, s, d = q.shape
    tq = NA_QROWS * GRID_W
    tkw = NA_KROWS * GRID_W
    nq = n_lat // tq
    return pl.pallas_call(
        functools.partial(_na_kernel, tq=tq, tkw=tkw, n_lat=n_lat, n_ctx=n_ctx, nq=nq),
        grid=(b, hq, nq),
        in_specs=[
            pl.BlockSpec((None, None, tq, d), lambda bi, h, i: (bi, h, i, 0)),
            pl.BlockSpec((None, None, s, d), lambda bi, h, i: (bi, h, 0, 0)),
            pl.BlockSpec((None, None, s, d), lambda bi, h, i: (bi, h, 0, 0)),
            pl.BlockSpec((None, 3, tq, tkw), lambda bi, h, i: (h, 0, 0, 0)),
            pl.BlockSpec(memory_space=pl.ANY),
        ],
        out_specs=pl.BlockSpec((None, None, tq, d), lambda bi, h, i: (bi, h, i, 0)),
        out_shape=jax.ShapeDtypeStruct((b, hq, s, d), MXU_DTYPE),
        input_output_aliases={4: 0},
        compiler_params=_cparams(("arbitrary", "arbitrary", "arbitrary")),
        name="neighbourhood_attention",
    )(q, k, v, tables, jnp.zeros((b, hq, s, d), MXU_DTYPE))


def _outproj_kernel(o1_ref, o2_ref, w_ref, x_ref, mod_ref, rw_ref, rb_ref,
                    xo_ref, h_ref, lg_ref, *, tm, n_lat, n_heads, transposed):
    row0 = pl.program_id(1) * tm
    if transposed:
        yt = jnp.concatenate([o1_ref[...].reshape(-1, tm), o2_ref[...].reshape(-1, tm)], axis=0)
        z = lax.dot_general(yt, w_ref[...], (((0,), (0,)), ((), ())), preferred_element_type=jnp.float32)
    else:
        y = jnp.concatenate([o1_ref[h] for h in range(n_heads)] + [o2_ref[h] for h in range(n_heads)], axis=-1)
        z = _dot(y, w_ref[...])
    rows = row0 + lax.broadcasted_iota(jnp.int32, (tm, 1), 0)
    gate = jnp.where(rows >= n_lat, mod_ref[6:7, :], mod_ref[4:5, :])
    xn = x_ref[...] + gate * z
    xo_ref[...] = xn
    h = _modulated_norm(xn, mod_ref, row0, n_lat, 0, 1, 2, 3)
    _store_row_tiles(h_ref, h)
    lg_ref[...] = jnp.dot(h, rw_ref[...], preferred_element_type=jnp.float32,
                          precision=lax.Precision.HIGHEST) + rb_ref[...]


def _outproj(o1, o2, w_out, xs, modv, router_w, router_b, n_lat, transposed):
    b, s, dm = xs.shape
    tm = ROW_TILE
    nh = o1.shape[1]
    nl = router_w.shape[1]
    if transposed:
        o_spec = pl.BlockSpec((None, nh, o1.shape[2], tm), lambda bi, j: (bi, 0, 0, j))
    else:
        o_spec = pl.BlockSpec((None, nh, tm, o1.shape[3]), lambda bi, j: (bi, 0, j, 0))
    return pl.pallas_call(
        functools.partial(_outproj_kernel, tm=tm, n_lat=n_lat, n_heads=nh, transposed=transposed),
        grid=(b, s // tm),
        in_specs=[
            o_spec,
            o_spec,
            pl.BlockSpec(w_out.shape, lambda bi, j: (0, 0)),
            pl.BlockSpec((None, tm, dm), lambda bi, j: (bi, j, 0)),
            pl.BlockSpec((None, 8, dm), lambda bi, j: (bi, 0, 0)),
            pl.BlockSpec(router_w.shape, lambda bi, j: (0, 0)),
            pl.BlockSpec(router_b.shape, lambda bi, j: (0, 0)),
        ],
        out_specs=[pl.BlockSpec((None, tm, dm), lambda bi, j: (bi, j, 0)),
                   pl.BlockSpec((None, tm * ROW_TILES, LANES), lambda bi, j: (bi, j, 0)),
                   pl.BlockSpec((None, tm, nl), lambda bi, j: (bi, j, 0))],
        out_shape=[jax.ShapeDtypeStruct((b, s, dm), jnp.float32),
                   jax.ShapeDtypeStruct((b, s * ROW_TILES, LANES), jnp.float32),
                   jax.ShapeDtypeStruct((b, s, nl), jnp.float32)],
        input_output_aliases={3: 0},
        compiler_params=_cparams(("arbitrary", "arbitrary")),
        name="outproj_residual_norm_router",
    )(o1, o2, w_out, xs, modv, router_w, router_b)


def _expert_kernel(be_ref, tok0_ref, tokn_ref, h_hbm, win_ref, bin_ref, wout_ref, bout_ref,
                   y_ref, xbuf, sems, wi_sc, wo_sc, *, blk, nb):
    i = pl.program_id(0)
    slot = lax.rem(i, 2)
    rt = ROW_TILES

    @pl.when(jnp.logical_or(i == 0, be_ref[i] != be_ref[jnp.maximum(i - 1, 0)]))
    def _():
        wi_sc[...] = win_ref[...].astype(wi_sc.dtype)
        wo_sc[...] = wout_ref[...].astype(wo_sc.dtype)

    def gather(tok_ref, dst_slot):
        for rr in range(blk):
            src = pl.multiple_of(tok_ref[0, rr], rt)
            pltpu.make_async_copy(h_hbm.at[pl.ds(src, rt)], xbuf.at[dst_slot, pl.ds(rr * rt, rt)],
                                  sems.at[dst_slot]).start()

    @pl.when(i == 0)
    def _():
        gather(tok0_ref, 0)

    @pl.when(i + 1 < nb)
    def _():
        gather(tokn_ref, 1 - slot)

    pltpu.make_async_copy(h_hbm.at[pl.ds(0, blk * rt)], xbuf.at[slot], sems.at[slot]).wait()

    x = _load_row_tiles(xbuf.at[slot], blk).astype(MXU_DTYPE)
    u = _dot(x, wi_sc[...]) + bin_ref[...]
    glu = jnp.minimum(u[:, :D_EXPERT], SWIGLU_LIMIT)
    lin = jnp.clip(u[:, D_EXPERT:], -SWIGLU_LIMIT, SWIGLU_LIMIT)
    act = glu * jax.nn.sigmoid(SWIGLU_ALPHA * glu) * (lin + 1.0)
    _store_row_tiles(y_ref, _dot(act.astype(MXU_DTYPE), wo_sc[...]) + bout_ref[...])


def _experts(h_tiles, blk_expert, tok_buf, w_in, b_in, w_out, b_out, layer):
    dm = w_in.shape[2]
    blk = MOE_ROWS
    rt = ROW_TILES
    nb = blk_expert.shape[0]
    de2 = w_in.shape[-1]
    tok3 = tok_buf.reshape(nb, 1, blk)
    grid_spec = pltpu.PrefetchScalarGridSpec(
        num_scalar_prefetch=1,
        grid=(nb,),
        in_specs=[
            pl.BlockSpec((None, 1, blk), lambda i, be: (0, 0, 0), memory_space=pltpu.SMEM),
            pl.BlockSpec((None, 1, blk), lambda i, be: (jnp.minimum(i + 1, nb - 1), 0, 0),
                         memory_space=pltpu.SMEM),
            pl.BlockSpec(memory_space=pl.ANY),
            pl.BlockSpec((None, None, dm, de2), lambda i, be: (layer, be[i], 0, 0)),
            pl.BlockSpec((None, 1, de2), lambda i, be: (be[i], 0, 0)),
            pl.BlockSpec((None, None, de2 // 2, dm), lambda i, be: (layer, be[i], 0, 0)),
            pl.BlockSpec((None, 1, dm), lambda i, be: (be[i], 0, 0)),
        ],
        out_specs=pl.BlockSpec((blk * rt, LANES), lambda i, be: (i, 0)),
        scratch_shapes=[pltpu.VMEM((2, blk * rt, LANES), jnp.float32), pltpu.SemaphoreType.DMA((2,)),
                        pltpu.VMEM((dm, de2), MXU_DTYPE), pltpu.VMEM((de2 // 2, dm), MXU_DTYPE)],
    )
    return pl.pallas_call(
        functools.partial(_expert_kernel, blk=blk, nb=nb),
        grid_spec=grid_spec,
        out_shape=jax.ShapeDtypeStruct((nb * blk * rt, LANES), jnp.float32),
        compiler_params=_cparams(("arbitrary",)),
        name="routed_experts",
    )(blk_expert, tok3, tok3, h_tiles, w_in, b_in.reshape(N_EXPERTS, 1, de2), w_out,
      b_out.reshape(N_EXPERTS, 1, dm))


def _combine_kernel(pos0_ref, posn_ref, y_hbm, x_ref, mod_ref, w_ref, o_ref, gbuf, sems, *, tc, nsteps, n_lat,
                    steps_per_batch):
    i = pl.program_id(0)
    slot = lax.rem(i, 2)

    rt = ROW_TILES

    def gather(pos_ref, dst_slot):
        def issue(t, carry):
            for kk in range(TOP_K):
                src = pl.multiple_of(pos_ref[0, t * TOP_K + kk], rt)
                dst = pl.multiple_of(t * rt, rt)
                pltpu.make_async_copy(y_hbm.at[pl.ds(src, rt)], gbuf.at[dst_slot, kk, pl.ds(dst, rt)],
                                      sems.at[dst_slot]).start()
            return carry
        lax.fori_loop(0, tc, issue, 0, unroll=8)

    @pl.when(i == 0)
    def _():
        gather(pos0_ref, 0)

    @pl.when(i + 1 < nsteps)
    def _():
        gather(posn_ref, 1 - slot)

    for kk in range(TOP_K):
        pltpu.make_async_copy(y_hbm.at[pl.ds(0, tc * rt)], gbuf.at[slot, kk], sems.at[slot]).wait()

    w = w_ref[...]
    y = _load_row_tiles(gbuf.at[slot, 0], tc) * w[:, 0:1]
    for kk in range(1, TOP_K):
        y = y + _load_row_tiles(gbuf.at[slot, kk], tc) * w[:, kk:kk + 1]
    row0 = lax.rem(i, steps_per_batch) * tc
    rows = row0 + lax.broadcasted_iota(jnp.int32, (tc, 1), 0)
    gate = jnp.where(rows >= n_lat, mod_ref[7:8, :], mod_ref[5:6, :])
    o_ref[...] = x_ref[...] + gate * y


def _combine(y_sorted, pos, weights, xs_flat, modv, s, n_lat):
    n, dm = xs_flat.shape
    tc = COMBINE_ROWS
    nsteps = n // tc
    spb = s // tc
    pos3 = pos.reshape(nsteps, 1, tc * TOP_K)
    return pl.pallas_call(
        functools.partial(_combine_kernel, tc=tc, nsteps=nsteps, n_lat=n_lat, steps_per_batch=spb),
        grid=(nsteps,),
        in_specs=[
            pl.BlockSpec((None, 1, tc * TOP_K), lambda i: (0, 0, 0), memory_space=pltpu.SMEM),
            pl.BlockSpec((None, 1, tc * TOP_K), lambda i: (jnp.minimum(i + 1, nsteps - 1), 0, 0),
                         memory_space=pltpu.SMEM),
            pl.BlockSpec(memory_space=pl.ANY),
            pl.BlockSpec((tc, dm), lambda i: (i, 0)),
            pl.BlockSpec((None, 8, dm), lambda i: (i // spb, 0, 0)),
            pl.BlockSpec((tc, TOP_K), lambda i: (i, 0)),
        ],
        out_specs=pl.BlockSpec((tc, dm), lambda i: (i, 0)),
        out_shape=jax.ShapeDtypeStruct((n, dm), jnp.float32),
        scratch_shapes=[pltpu.VMEM((2, TOP_K, tc * ROW_TILES, LANES), jnp.float32), pltpu.SemaphoreType.DMA((2,))],
        input_output_aliases={3: 0},
        compiler_params=_cparams(("arbitrary",)),
        name="expert_combine",
    )(pos3, pos3, y_sorted, xs_flat, modv, weights)


def _final_norm_kernel(x_ref, g_ref, o_ref):
    x = x_ref[...]
    o_ref[...] = x * lax.rsqrt(jnp.mean(x * x, axis=-1, keepdims=True) + NORM_EPS) * g_ref[...]


def _final_norm(xs, g, n_lat):
    b, s, dm = xs.shape
    tm = 512
    return pl.pallas_call(
        _final_norm_kernel,
        grid=(b, n_lat // tm),
        in_specs=[pl.BlockSpec((None, tm, dm), lambda bi, j: (bi, j, 0)),
                  pl.BlockSpec((1, dm), lambda bi, j: (0, 0))],
        out_specs=pl.BlockSpec((None, tm, dm), lambda bi, j: (bi, j, 0)),
        out_shape=jax.ShapeDtypeStruct((b, n_lat, dm), jnp.float32),
        compiler_params=_cparams(("arbitrary", "arbitrary")),
        name="final_norm",
    )(xs, g.reshape(1, dm))


def _swap_perm(width, dim):
    j = np.arange(width)
    half = dim // 2
    return np.where((j % dim) < half, j + half, j - half)


def _rope_tables(n_lat, n_ctx, dim):
    t = jnp.arange(n_lat, dtype=jnp.int32)
    row = (t // GRID_W).astype(jnp.float32)
    col = (t % GRID_W).astype(jnp.float32)
    quarter = dim // 4
    inv_freq = ROPE_BASE ** (-jnp.arange(quarter, dtype=jnp.float32) / quarter)
    ang = jnp.concatenate([row[:, None] * inv_freq, col[:, None] * inv_freq], axis=-1)
    cos, sin = jnp.cos(ang), jnp.sin(ang)
    reps = LANES // dim
    cos_t = jnp.tile(jnp.concatenate([cos, cos], axis=-1), (1, reps))
    sin_t = jnp.tile(jnp.concatenate([-sin, sin], axis=-1), (1, reps))
    cos_t = jnp.concatenate([cos_t, jnp.ones((n_ctx, LANES), jnp.float32)], axis=0)
    sin_t = jnp.concatenate([sin_t, jnp.zeros((n_ctx, LANES), jnp.float32)], axis=0)
    return cos_t, sin_t


def _mod_rows(mod_l, norm_g, b, first):
    lat, ctx = mod_l[:b], mod_l[b]
    a_lat = norm_g[None] * (1.0 + lat[:, first + 1])
    b_lat = lat[:, first]
    a_ctx = jnp.broadcast_to(norm_g * (1.0 + ctx[first + 1]), a_lat.shape)
    b_ctx = jnp.broadcast_to(ctx[first], a_lat.shape)
    g_mix_ctx = jnp.broadcast_to(ctx[2], a_lat.shape)
    g_ffn_ctx = jnp.broadcast_to(ctx[5], a_lat.shape)
    return jnp.stack([a_lat, b_lat, a_ctx, b_ctx, lat[:, 2], lat[:, 5], g_mix_ctx, g_ffn_ctx], axis=1)


def _route(logits, blk):
    n = logits.shape[0]
    nk = n * TOP_K
    top_val, top_idx = lax.top_k(logits, TOP_K)
    gate = jax.nn.softmax(top_val, axis=-1)
    expert = top_idx.reshape(-1)
    onehot = (expert[:, None] == jnp.arange(N_EXPERTS, dtype=jnp.int32)[None, :]).astype(jnp.int32)
    csum = jnp.cumsum(onehot, axis=0)
    rank = jnp.take_along_axis(csum, expert[:, None], axis=1)[:, 0] - 1
    sizes = csum[-1]
    padded = (sizes + blk - 1) // blk * blk
    pends = jnp.cumsum(padded)
    pstarts = pends - padded
    dest = pstarts[expert] + rank
    nb = -(-(nk + N_EXPERTS * (blk - 1)) // blk)
    blk_expert = jnp.minimum(jnp.searchsorted(pends, jnp.arange(nb, dtype=jnp.int32) * blk, side='right'),
                             N_EXPERTS - 1).astype(jnp.int32)
    order = jnp.argsort(expert)
    starts = jnp.cumsum(sizes) - sizes
    slot = jnp.arange(nb * blk, dtype=jnp.int32)
    slot_expert = jnp.repeat(blk_expert, blk)
    slot_rank = slot - pstarts[slot_expert]
    src = jnp.minimum(starts[slot_expert] + slot_rank, nk - 1)
    tok_buf = jnp.where(slot_rank < sizes[slot_expert], order[src] // TOP_K, 0).astype(jnp.int32)
    return blk_expert, tok_buf, gate, dest.astype(jnp.int32)


def kernel(x, c, ctx, c_ctx, mod_w, mod_b, norm_mix, norm_ffn, ab_w_in, ab_w_out, a_sink, b_rpb,
           cd_w_in, c_q_norm, c_w_q_b, c_kv_norm, c_w_kv_b, d_q_norm, d_k_norm, cd_w_out,
           router_w, router_b, exp_w_in, exp_b_in, exp_w_out, exp_b_out, final_norm):
    b, t, dm = x.shape
    assert dm == ROW_TILES * LANES
    n_ctx = ctx.shape[1]
    s = t + n_ctx
    depth = mod_w.shape[0]
    rows = t // GRID_W
    d = HEAD_DIM
    f32 = jnp.float32

    xs = jnp.concatenate([x, ctx], axis=1)

    cond = jnp.zeros((8, dm), f32).at[:b].set(c).at[b].set(c_ctx)
    mod_all = _modulation(cond, mod_w, mod_b)[:, :b + 1].reshape(depth, b + 1, 6, dm)

    cos_t, sin_t = _rope_tables(t, n_ctx, d)
    cosm_t, sinm_t = _rope_tables(t, n_ctx, C_ROPE)

    router_w_p = jnp.zeros((depth, dm, LANES), f32).at[:, :, :N_EXPERTS].set(router_w)
    router_b_p = jnp.zeros((depth, 1, LANES), f32).at[:, 0, :N_EXPERTS].set(router_b)

    p64_512 = _swap_perm(512, d)
    p64_128 = _swap_perm(128, d)
    p32 = _swap_perm(C_ROPE, C_ROPE)

    for layer in range(depth):
        i = layer // 2
        modv_mix = _mod_rows(mod_all[layer], norm_mix[layer], b, 0)
        modv_ffn = _mod_rows(mod_all[layer], norm_ffn[layer], b, 3)
        if layer % 2 == 0:
            w = ab_w_in[i]
            w_ext = jnp.concatenate([w, w[:, 0:512][:, p64_512], w[:, 512:640][:, p64_128]], axis=1).astype(MXU_DTYPE)
            qa, ka, va, qb, kb, vb = _proj_ab(xs, modv_mix, w_ext, cos_t, sin_t, t)
            sink_gr = a_sink[i].astype(f32).reshape(A_KV_HEADS, A_HEADS // A_KV_HEADS)
            r = A_HEADS // A_KV_HEADS
            sink_win = jnp.repeat(sink_gr, WIN_Q, axis=1).reshape(A_KV_HEADS, r * WIN_Q, 1)
            sink_ctx = jnp.repeat(sink_gr, n_ctx, axis=1).reshape(A_KV_HEADS, r * n_ctx, 1)
            o1 = _window_attention(qa, ka, va, sink_win, t, n_ctx)
            o1 = _flash(qa, ka, va, groups=A_KV_HEADS, tq=n_ctx, tk=n_ctx, q_row0=t, n_q=n_ctx,
                        kv_row0=t, n_kv=n_ctx, sink=sink_ctx, out=o1)
            tables = _na_tables(b_rpb[i], rows)
            o2 = _neighbourhood_attention(qb, kb, vb, tables, t, n_ctx)
            o2 = _flash(qb, kb, vb, groups=B_HEADS, tq=n_ctx, tk=n_ctx, q_row0=t, n_q=n_ctx,
                        kv_row0=t, n_kv=n_ctx, out=o2)
            w_out = ab_w_out[i].astype(MXU_DTYPE)
        else:
            w = cd_w_in[i]
            base = C_Q_RANK + C_KV_RANK
            o_qd, o_kd, o_vd = base + C_ROPE, base + C_ROPE + 512, base + C_ROPE + 640
            wqd, wkd = w[:, o_qd:o_qd + 512], w[:, o_kd:o_kd + 128]
            gq = jnp.tile(d_q_norm[i].astype(f32), D_HEADS)
            gk = jnp.tile(d_k_norm[i].astype(f32), D_KV_HEADS)
            krope = w[:, base:base + C_ROPE]
            pad = jnp.zeros((dm, LANES - C_ROPE), f32)
            w1 = jnp.concatenate([
                w[:, 0:base], wqd, wkd, w[:, o_vd:o_vd + 128],
                (wqd * gq[None])[:, p64_512], (wkd * gk[None])[:, p64_128],
                krope, pad, krope[:, p32], pad], axis=1).astype(MXU_DTYPE)
            dq = C_NOPE + C_ROPE
            wq = c_w_q_b[i].reshape(C_Q_RANK, C_HEADS, dq)
            wq_nope = wq[:, :, :C_NOPE].reshape(C_Q_RANK, C_HEADS * C_NOPE)
            wq_rope = wq[:, :, C_NOPE:].reshape(C_Q_RANK, C_HEADS * C_ROPE)
            wq2 = jnp.concatenate([wq_nope, wq_rope, wq_rope[:, _swap_perm(C_HEADS * C_ROPE, C_ROPE)]],
                                  axis=1).astype(MXU_DTYPE)
            wkv = c_w_kv_b[i].reshape(C_KV_RANK, C_HEADS, C_NOPE + C_V)
            wkv2 = jnp.concatenate([wkv[:, :, :C_NOPE].reshape(C_KV_RANK, -1),
                                    wkv[:, :, C_NOPE:].reshape(C_KV_RANK, -1)], axis=1).astype(MXU_DTYPE)
            qc, kc, vc, qd, kd, vd = _proj_cd(
                xs, modv_mix, w1, wq2, wkv2, c_q_norm[i].astype(f32).reshape(1, -1),
                c_kv_norm[i].astype(f32).reshape(1, -1), gq.reshape(1, -1), gk.reshape(1, -1),
                cos_t, sin_t, cosm_t, sinm_t, t)
            o1 = _flash_t(qc, kc, vc, groups=C_HEADS, tq=FLASH_M, tk=FLASH_TK, q_row0=0, n_q=t, kv_row0=0, n_kv=s,
                          out=jnp.zeros((b, C_HEADS, C_V, s), MXU_DTYPE))
            o1 = _flash_t(qc, kc, vc, groups=C_HEADS, tq=n_ctx, tk=n_ctx, q_row0=t, n_q=n_ctx,
                          kv_row0=t, n_kv=n_ctx, out=o1)
            rd = D_HEADS // D_KV_HEADS
            o2 = _flash_t(qd, kd, vd, groups=D_KV_HEADS, tq=FLASH_M // rd, tk=FLASH_TK, q_row0=0, n_q=t,
                          kv_row0=0, n_kv=s, out=jnp.zeros((b, D_HEADS, d, s), MXU_DTYPE))
            o2 = _flash_t(qd, kd, vd, groups=D_KV_HEADS, tq=n_ctx, tk=n_ctx, q_row0=t, n_q=n_ctx,
                          kv_row0=t, n_kv=n_ctx, out=o2)
            w_out = cd_w_out[i].astype(MXU_DTYPE)

        xs, h2, logits = _outproj(o1, o2, w_out, xs, modv_ffn, router_w_p[layer], router_b_p[layer], t,
                                  transposed=layer % 2 == 1)

        n = b * s
        blk_expert, tok_buf, weights, pos = _route(logits.reshape(n, LANES)[:, :N_EXPERTS], MOE_ROWS)
        y_sorted = _experts(h2.reshape(n * ROW_TILES, LANES), blk_expert, tok_buf * ROW_TILES,
                            exp_w_in.astype(f32), exp_b_in[layer].astype(f32),
                            exp_w_out.astype(f32), exp_b_out[layer].astype(f32), layer)
        xs = _combine(y_sorted, pos * ROW_TILES, weights, xs.reshape(n, dm), modv_ffn, s, t).reshape(b, s, dm)

    return _final_norm(xs, final_norm.astype(f32), t)
```

```python
import functools

import jax
import jax.numpy as jnp
import numpy as np
from jax import lax
from jax.experimental import pallas as pl
from jax.experimental.pallas import tpu as pltpu

GRID_W = 64
HEAD_DIM = 64
ROPE_BASE = 10000.0
NORM_EPS = 1e-6
NEG_INF = -1e30
A_HEADS = 8
A_KV_HEADS = 2
A_WINDOW = 128
B_HEADS = 8
NA_ROWS = 8
NA_COLS = 16
C_HEADS = 8
C_Q_RANK = 768
C_KV_RANK = 256
C_NOPE = 64
C_ROPE = 32
C_V = 64
D_HEADS = 8
D_KV_HEADS = 2
N_EXPERTS = 32
TOP_K = 4
D_EXPERT = 1024
SWIGLU_LIMIT = 7.0
SWIGLU_ALPHA = 1.702
LOG2_E = 1.4426950408889634

LANES = 128
BF16_SUBLANES = 16
ROW_TILES = 8
VMEM_LIMIT_BYTES = 56 * 1024 * 1024

MXU_DTYPE = jnp.bfloat16

ROW_TILE = 640
MOE_ROWS = 256
COMBINE_ROWS = 256
NA_QROWS = 8
NA_KROWS = 16
WIN_Q = 256
ATTN_SPLIT = 2
FLASH_M = 1024
FLASH_TK = 640
FLASH_UNROLL = 8


def _cparams(sem, vmem=VMEM_LIMIT_BYTES):
    return pltpu.CompilerParams(dimension_semantics=sem, vmem_limit_bytes=vmem)


def _dot(a, b):
    return jnp.dot(a, b, preferred_element_type=jnp.float32)


def _dot_nt(a, b):
    return lax.dot_general(a, b, (((1,), (1,)), ((), ())), preferred_element_type=jnp.float32)


def _mod_kernel(c_ref, w_ref, b_ref, o_ref):
    c = c_ref[...]
    s = c * jax.nn.sigmoid(c)
    o_ref[...] = jnp.dot(s, w_ref[...], preferred_element_type=jnp.float32,
                         precision=lax.Precision.HIGHEST) + b_ref[...]


def _modulation(cond, mod_w, mod_b):
    depth, dm, n = mod_w.shape
    tn = 1536
    return pl.pallas_call(
        _mod_kernel,
        grid=(depth, n // tn),
        in_specs=[
            pl.BlockSpec((8, dm), lambda l, j: (0, 0)),
            pl.BlockSpec((None, dm, tn), lambda l, j: (l, 0, j)),
            pl.BlockSpec((None, 1, tn), lambda l, j: (l, 0, j)),
        ],
        out_specs=pl.BlockSpec((None, 8, tn), lambda l, j: (l, 0, j)),
        out_shape=jax.ShapeDtypeStruct((depth, 8, n), jnp.float32),
        compiler_params=_cparams(("arbitrary", "arbitrary")),
        name="adaln_modulation",
    )(cond, mod_w, mod_b.reshape(depth, 1, n))


def _modulated_norm(x, mod_ref, row0, n_lat, a_lat, b_lat, a_ctx, b_ctx):
    tm = x.shape[0]
    ms = jnp.mean(x * x, axis=-1, keepdims=True)
    xh = x * lax.rsqrt(ms + NORM_EPS)
    rows = row0 + lax.broadcasted_iota(jnp.int32, (tm, 1), 0)
    is_ctx = rows >= n_lat
    a = jnp.where(is_ctx, mod_ref[a_ctx:a_ctx + 1, :], mod_ref[a_lat:a_lat + 1, :])
    b = jnp.where(is_ctx, mod_ref[b_ctx:b_ctx + 1, :], mod_ref[b_lat:b_lat + 1, :])
    return xh * a + b


def _store_heads(o_ref, val, n_heads, width, lane0=0):
    for h in range(n_heads):
        o_ref[h, :, lane0:lane0 + width] = val[:, h * width:(h + 1) * width].astype(o_ref.dtype)


def _store_row_tiles(o_ref, val):
    rows = val.shape[0]
    for c in range(ROW_TILES):
        o_ref[pl.ds(c, rows, stride=ROW_TILES), :] = val[:, c * LANES:(c + 1) * LANES]


def _load_row_tiles(ref, rows):
    return jnp.concatenate([ref[pl.ds(c, rows, stride=ROW_TILES), :] for c in range(ROW_TILES)], axis=-1)


def _tile_lanes(t, reps):
    return t if reps == 1 else jnp.concatenate([t] * reps, axis=-1)


def _proj_ab_kernel(x_ref, mod_ref, w_ref, cos_ref, sin_ref,
                    qa_ref, ka_ref, va_ref, qb_ref, kb_ref, vb_ref, *, tm, n_lat):
    row0 = pl.program_id(1) * tm
    h = _modulated_norm(x_ref[...], mod_ref, row0, n_lat, 0, 1, 2, 3).astype(MXU_DTYPE)
    cos = cos_ref[...]
    sin = sin_ref[...]
    d = HEAD_DIM
    scale = d ** -0.5
    qa = _dot(h, w_ref[:, 0:512]) * _tile_lanes(cos, 4) + _dot(h, w_ref[:, 2304:2816]) * _tile_lanes(sin, 4)
    _store_heads(qa_ref, qa * scale, A_HEADS, d)
    ka = _dot(h, w_ref[:, 512:640]) * cos + _dot(h, w_ref[:, 2816:2944]) * sin
    _store_heads(ka_ref, ka, A_KV_HEADS, d)
    _store_heads(va_ref, _dot(h, w_ref[:, 640:768]), A_KV_HEADS, d)
    _store_heads(qb_ref, _dot(h, w_ref[:, 768:1280]) * scale, B_HEADS, d)
    _store_heads(kb_ref, _dot(h, w_ref[:, 1280:1792]), B_HEADS, d)
    _store_heads(vb_ref, _dot(h, w_ref[:, 1792:2304]), B_HEADS, d)


def _proj_ab(xs, modv, w_ext, cos_t, sin_t, n_lat):
    b, s, dm = xs.shape
    tm = ROW_TILE
    d = HEAD_DIM
    nw = w_ext.shape[1]

    def head_spec(nh):
        return pl.BlockSpec((None, nh, tm, d), lambda bi, j: (bi, 0, j, 0))

    def head_shape(nh):
        return jax.ShapeDtypeStruct((b, nh, s, d), MXU_DTYPE)

    return pl.pallas_call(
        functools.partial(_proj_ab_kernel, tm=tm, n_lat=n_lat),
        grid=(b, s // tm),
        in_specs=[
            pl.BlockSpec((None, tm, dm), lambda bi, j: (bi, j, 0)),
            pl.BlockSpec((None, 8, dm), lambda bi, j: (bi, 0, 0)),
            pl.BlockSpec((dm, nw), lambda bi, j: (0, 0)),
            pl.BlockSpec((tm, LANES), lambda bi, j: (j, 0)),
            pl.BlockSpec((tm, LANES), lambda bi, j: (j, 0)),
        ],
        out_specs=[head_spec(A_HEADS), head_spec(A_KV_HEADS), head_spec(A_KV_HEADS),
                   head_spec(B_HEADS), head_spec(B_HEADS), head_spec(B_HEADS)],
        out_shape=[head_shape(A_HEADS), head_shape(A_KV_HEADS), head_shape(A_KV_HEADS),
                   head_shape(B_HEADS), head_shape(B_HEADS), head_shape(B_HEADS)],
        compiler_params=_cparams(("arbitrary", "arbitrary")),
        name="proj_ab",
    )(xs, modv, w_ext, cos_t, sin_t)


def _proj_cd_kernel(x_ref, mod_ref, w1_ref, wq2_ref, wkv2_ref, qn_ref, kvn_ref, gq_ref, gk_ref,
                    cos_ref, sin_ref, cosm_ref, sinm_ref,
                    qc_ref, kc_ref, vc_ref, qd_ref, kd_ref, vd_ref, *, tm, n_lat):
    row0 = pl.program_id(1) * tm
    h = _modulated_norm(x_ref[...], mod_ref, row0, n_lat, 0, 1, 2, 3).astype(MXU_DTYPE)
    cos = cos_ref[...]
    sin = sin_ref[...]
    cosm = cosm_ref[...]
    sinm = sinm_ref[...]
    d = HEAD_DIM

    cq = _dot(h, w1_ref[:, 0:768])
    ckv = _dot(h, w1_ref[:, 768:1024])
    qd_raw, qd_swp = _dot(h, w1_ref[:, 1024:1536]), _dot(h, w1_ref[:, 1792:2304])
    kd_raw, kd_swp = _dot(h, w1_ref[:, 1536:1664]), _dot(h, w1_ref[:, 2304:2432])
    vd = _dot(h, w1_ref[:, 1664:1792])
    kr_raw, kr_swp = _dot(h, w1_ref[:, 2432:2560]), _dot(h, w1_ref[:, 2560:2688])

    cqn = cq * lax.rsqrt(jnp.mean(cq * cq, axis=-1, keepdims=True) + NORM_EPS) * qn_ref[...]
    ckvn = ckv * lax.rsqrt(jnp.mean(ckv * ckv, axis=-1, keepdims=True) + NORM_EPS) * kvn_ref[...]
    q2 = _dot(cqn.astype(MXU_DTYPE), wq2_ref[...])
    kv2 = _dot(ckvn.astype(MXU_DTYPE), wkv2_ref[...])

    def normed_rope(raw, swapped_gained, gain, n_heads, o_ref, scale):
        reps = n_heads * d // LANES
        val = raw * (_tile_lanes(cos, reps) * gain) + swapped_gained * _tile_lanes(sin, reps)
        for hh in range(n_heads):
            sl = raw[:, hh * d:(hh + 1) * d]
            r = lax.rsqrt(jnp.mean(sl * sl, axis=-1, keepdims=True) + NORM_EPS)
            o_ref[hh, :, :] = (val[:, hh * d:(hh + 1) * d] * (r * scale)).astype(o_ref.dtype)

    normed_rope(qd_raw, qd_swp, gq_ref[...], D_HEADS, qd_ref, d ** -0.5 * LOG2_E)
    normed_rope(kd_raw, kd_swp, gk_ref[...], D_KV_HEADS, kd_ref, 1.0)
    _store_heads(vd_ref, vd, D_KV_HEADS, d)

    c_scale = (C_NOPE + C_ROPE) ** -0.5 * LOG2_E
    q_rope = (q2[:, 512:768] * _tile_lanes(cosm, 2) + q2[:, 768:1024] * _tile_lanes(sinm, 2)) * c_scale
    _store_heads(qc_ref, q2[:, 0:512] * c_scale, C_HEADS, C_NOPE)
    _store_heads(qc_ref, q_rope, C_HEADS, C_ROPE, lane0=C_NOPE)
    k_rope = (kr_raw * cosm + kr_swp * sinm)[:, 0:C_ROPE]
    _store_heads(kc_ref, kv2[:, 0:512], C_HEADS, C_NOPE)
    for hh in range(C_HEADS):
        kc_ref[hh, :, C_NOPE:C_NOPE + C_ROPE] = k_rope.astype(kc_ref.dtype)
    _store_heads(vc_ref, kv2[:, 512:1024], C_HEADS, C_V)


def _proj_cd(xs, modv, w1, wq2, wkv2, qn, kvn, gq, gk, cos_t, sin_t, cosm_t, sinm_t, n_lat):
    b, s, dm = xs.shape
    tm = ROW_TILE
    dk_c = C_NOPE + C_ROPE

    def full(a):
        return pl.BlockSpec(a.shape, lambda bi, j: (0,) * a.ndim)

    def head_spec(nh, w):
        return pl.BlockSpec((None, nh, tm, w), lambda bi, j: (bi, 0, j, 0))

    def head_shape(nh, w):
        return jax.ShapeDtypeStruct((b, nh, s, w), MXU_DTYPE)

    tab = pl.BlockSpec((tm, LANES), lambda bi, j: (j, 0))
    return pl.pallas_call(
        functools.partial(_proj_cd_kernel, tm=tm, n_lat=n_lat),
        grid=(b, s // tm),
        in_specs=[
            pl.BlockSpec((None, tm, dm), lambda bi, j: (bi, j, 0)),
            pl.BlockSpec((None, 8, dm), lambda bi, j: (bi, 0, 0)),
            full(w1), full(wq2), full(wkv2), full(qn), full(kvn), full(gq), full(gk),
            tab, tab, tab, tab,
        ],
        out_specs=[head_spec(C_HEADS, dk_c), head_spec(C_HEADS, dk_c), head_spec(C_HEADS, C_V),
                   head_spec(D_HEADS, HEAD_DIM), head_spec(D_KV_HEADS, HEAD_DIM), head_spec(D_KV_HEADS, HEAD_DIM)],
        out_shape=[head_shape(C_HEADS, dk_c), head_shape(C_HEADS, dk_c), head_shape(C_HEADS, C_V),
                   head_shape(D_HEADS, HEAD_DIM), head_shape(D_KV_HEADS, HEAD_DIM), head_shape(D_KV_HEADS, HEAD_DIM)],
        compiler_params=_cparams(("arbitrary", "arbitrary")),
        name="proj_cd",
    )(xs, modv, w1, wq2, wkv2, qn, kvn, gq, gk, cos_t, sin_t, cosm_t, sinm_t)


def _flash_kernel(*refs, r, tq, tk, nk, has_sink):
    if has_sink:
        q_ref, k_ref, v_ref, sink_ref, o_ref, m_sc, l_sc, acc_sc = refs
    else:
        q_ref, k_ref, v_ref, o_ref, m_sc, l_sc, acc_sc = refs
    m_rows = r * tq
    q = q_ref[...].reshape(m_rows, q_ref.shape[-1])
    if has_sink:
        m_sc[...] = sink_ref[...]
        l_sc[...] = jnp.ones_like(l_sc)
    else:
        m_sc[...] = jnp.full_like(m_sc, NEG_INF)
        l_sc[...] = jnp.zeros_like(l_sc)
    acc_sc[...] = jnp.zeros_like(acc_sc)

    def body(j, carry):
        k0 = pl.multiple_of(j * tk, tk)
        k = k_ref[pl.ds(k0, tk), :]
        v = v_ref[pl.ds(k0, tk), :]
        s = _dot_nt(q, k)
        m_prev = m_sc[...]
        m_new = jnp.maximum(m_prev, jnp.max(s, axis=-1, keepdims=True))
        alpha = jnp.exp(m_prev - m_new)
        p = jnp.exp(s - m_new)
        l_sc[...] = alpha * l_sc[...] + jnp.sum(p, axis=-1, keepdims=True)
        acc_sc[...] = alpha * acc_sc[...] + _dot(p.astype(MXU_DTYPE), v)
        m_sc[...] = m_new
        return carry

    lax.fori_loop(0, nk, body, 0)
    o = acc_sc[...] / l_sc[...]
    o_ref[...] = o.reshape(o_ref.shape).astype(o_ref.dtype)


def _flash(q, k, v, *, groups, tq, tk, q_row0, n_q, kv_row0, n_kv, sink=None, out=None):
    b, hq, s, dk = q.shape
    dv = v.shape[-1]
    r = hq // groups
    q5 = q.reshape(b, groups, r, s, dk)
    nk = n_kv // tk
    has_sink = sink is not None
    m_rows = r * tq
    in_specs = [
        pl.BlockSpec((None, None, r, tq, dk), lambda bi, g, i: (bi, g, 0, q_row0 // tq + i, 0)),
        pl.BlockSpec((None, None, n_kv, dk), lambda bi, g, i: (bi, g, kv_row0 // n_kv, 0)),
        pl.BlockSpec((None, None, n_kv, dv), lambda bi, g, i: (bi, g, kv_row0 // n_kv, 0)),
    ]
    args = [q5, k, v]
    if has_sink:
        in_specs.append(pl.BlockSpec((None, m_rows, 1), lambda bi, g, i: (g, 0, 0)))
        args.append(sink)
    aliases = {}
    if out is not None:
        in_specs.append(pl.BlockSpec(memory_space=pl.ANY))
        args.append(out.reshape(b, groups, r, s, dv))
        aliases = {len(args) - 1: 0}
    kern = functools.partial(_flash_kernel, r=r, tq=tq, tk=tk, nk=nk, has_sink=has_sink)
    if out is not None:
        inner = kern
        kern = lambda *refs: inner(*refs[:len(args) - 1], *refs[len(args):])
    o = pl.pallas_call(
        kern,
        grid=(b, groups, n_q // tq),
        in_specs=in_specs,
        out_specs=pl.BlockSpec((None, None, r, tq, dv), lambda bi, g, i: (bi, g, 0, q_row0 // tq + i, 0)),
        out_shape=jax.ShapeDtypeStruct((b, groups, r, s, dv), MXU_DTYPE),
        scratch_shapes=[pltpu.VMEM((m_rows, 1), jnp.float32), pltpu.VMEM((m_rows, 1), jnp.float32),
                        pltpu.VMEM((m_rows, dv), jnp.float32)],
        input_output_aliases=aliases,
        compiler_params=_cparams(("arbitrary", "arbitrary", "arbitrary")),
        name="dense_attention",
    )(*args)
    return o.reshape(b, hq, s, dv)


def _flash_t_kernel(q_ref, k_ref, vt_ref, init_ref, o_ref, m_sc, acc_sc, sa_sc, sb_sc, *, r, tq, tk, nk, dv):
    del init_ref
    m_cols = r * tq
    q = q_ref[...].reshape(m_cols, q_ref.shape[-1])
    m_sc[...] = jnp.full_like(m_sc, NEG_INF)
    acc_sc[...] = jnp.zeros_like(acc_sc)
    bufs = (sa_sc, sb_sc)

    def scores(j, s_ref):
        k0 = pl.multiple_of(j * tk, tk)
        s_ref[...] = _dot_nt(k_ref[pl.ds(k0, tk), :], q)

    def update(j, s_ref):
        st = s_ref[...]
        m_prev = m_sc[...]
        m_new = jnp.maximum(m_prev, jnp.max(st, axis=0, keepdims=True))
        alpha = jnp.exp2(m_prev - m_new)
        pt = jnp.exp2(st - m_new)
        acc_sc[...] = alpha * acc_sc[...] + _dot(vt_ref[j], pt.astype(MXU_DTYPE))
        m_sc[...] = m_new

    scores(0, sa_sc)
    trips = (nk - 1) // FLASH_UNROLL

    def body(jj, carry):
        j = FLASH_UNROLL * jj
        for c in range(FLASH_UNROLL):
            scores(j + c + 1, bufs[(c + 1) % 2])
            update(j + c, bufs[c % 2])
        return carry

    lax.fori_loop(0, trips, body, 0)
    for j in range(trips * FLASH_UNROLL, nk):
        if j + 1 < nk:
            scores(j + 1, bufs[(j + 1) % 2])
        update(j, bufs[j % 2])
    o = acc_sc[0:dv, :] / acc_sc[dv:dv + 1, :]
    for rr in range(r):
        o_ref[rr] = o[:, rr * tq:(rr + 1) * tq].astype(o_ref.dtype)


def _flash_t(q, k, v, *, groups, tq, tk, q_row0, n_q, kv_row0, n_kv, out):
    b, hq, s, dk = q.shape
    dv = v.shape[-1]
    r = hq // groups
    q5 = q.reshape(b, groups, r, s, dk)
    nk = n_kv // tk
    vt = jnp.swapaxes(lax.slice_in_dim(v, kv_row0, kv_row0 + n_kv, axis=2).reshape(b, groups, nk, tk, dv), 3, 4)
    dva = dv + BF16_SUBLANES
    vt = jnp.concatenate([vt, jnp.ones((b, groups, nk, BF16_SUBLANES, tk), vt.dtype)], axis=3)
    m_cols = r * tq
    o = pl.pallas_call(
        functools.partial(_flash_t_kernel, r=r, tq=tq, tk=tk, nk=nk, dv=dv),
        grid=(b, groups, n_q // tq),
        in_specs=[
            pl.BlockSpec((None, None, r, tq, dk), lambda bi, g, i: (bi, g, 0, q_row0 // tq + i, 0)),
            pl.BlockSpec((None, None, n_kv, dk), lambda bi, g, i: (bi, g, kv_row0 // n_kv, 0)),
            pl.BlockSpec((None, None, nk, dva, tk), lambda bi, g, i: (bi, g, 0, 0, 0)),
            pl.BlockSpec(memory_space=pl.ANY),
        ],
        out_specs=pl.BlockSpec((None, None, r, dv, tq), lambda bi, g, i: (bi, g, 0, 0, q_row0 // tq + i)),
        out_shape=jax.ShapeDtypeStruct((b, groups, r, dv, s), MXU_DTYPE),
        scratch_shapes=[pltpu.VMEM((1, m_cols), jnp.float32),
                        pltpu.VMEM((dva, m_cols), jnp.float32),
                        pltpu.VMEM((tk, m_cols), jnp.float32), pltpu.VMEM((tk, m_cols), jnp.float32)],
        input_output_aliases={3: 0},
        compiler_params=_cparams(("arbitrary", "arbitrary", "arbitrary")),
        name="dense_attention_t",
    )(q5, k, vt, out.reshape(b, groups, r, dv, s))
    return o.reshape(b, hq, dv, s)


def _window_kernel(q_ref, k_ref, v_ref, sink_ref, init_ref, o_ref, *, r, tq, n_lat, n_ctx):
    del init_ref
    i = pl.program_id(2)
    span = tq + 2 * A_WINDOW
    d = q_ref.shape[-1]
    start = pl.multiple_of(jnp.clip(i * tq - A_WINDOW, 0, n_lat - span), A_WINDOW)
    k_loc = k_ref[pl.ds(start, span), :]
    v_loc = v_ref[pl.ds(start, span), :]
    k_ctx = k_ref[n_lat:n_lat + n_ctx, :]
    v_ctx = v_ref[n_lat:n_lat + n_ctx, :]
    hp = r // ATTN_SPLIT
    rows = hp * tq
    qs = [q_ref[p * hp:(p + 1) * hp].reshape(rows, d) for p in range(ATTN_SPLIT)]
    scores = [(_dot_nt(qp, k_loc), _dot_nt(qp, k_ctx)) for qp in qs]
    qpos = i * tq + lax.rem(lax.broadcasted_iota(jnp.int32, (rows, span), 0), tq)
    kpos = start + lax.broadcasted_iota(jnp.int32, (rows, span), 1)
    ok = jnp.abs(qpos - kpos) <= A_WINDOW
    for p, (s_loc, s_ctx) in enumerate(scores):
        s_loc = jnp.where(ok, s_loc, NEG_INF)
        sink = sink_ref[p * rows:(p + 1) * rows, :]
        m = jnp.maximum(jnp.maximum(jnp.max(s_loc, axis=-1, keepdims=True),
                                    jnp.max(s_ctx, axis=-1, keepdims=True)), sink)
        p_loc = jnp.exp(s_loc - m)
        p_ctx = jnp.exp(s_ctx - m)
        denom = (jnp.sum(p_loc, axis=-1, keepdims=True) + jnp.sum(p_ctx, axis=-1, keepdims=True)
                 + jnp.exp(sink - m))
        o = (_dot(p_loc.astype(MXU_DTYPE), v_loc) + _dot(p_ctx.astype(MXU_DTYPE), v_ctx)) / denom
        o_ref[p * hp:(p + 1) * hp] = o.reshape(hp, tq, d).astype(o_ref.dtype)


def _window_attention(q, k, v, sink_col, n_lat, n_ctx):
    b, hq, s, d = q.shape
    g = k.shape[1]
    r = hq // g
    tq = WIN_Q
    q5 = q.reshape(b, g, r, s, d)
    o = pl.pallas_call(
        functools.partial(_window_kernel, r=r, tq=tq, n_lat=n_lat, n_ctx=n_ctx),
        grid=(b, g, n_lat // tq),
        in_specs=[
            pl.BlockSpec((None, None, r, tq, d), lambda bi, gi, i: (bi, gi, 0, i, 0)),
            pl.BlockSpec((None, None, s, d), lambda bi, gi, i: (bi, gi, 0, 0)),
            pl.BlockSpec((None, None, s, d), lambda bi, gi, i: (bi, gi, 0, 0)),
            pl.BlockSpec((None, r * tq, 1), lambda bi, gi, i: (gi, 0, 0)),
            pl.BlockSpec(memory_space=pl.ANY),
        ],
        out_specs=pl.BlockSpec((None, None, r, tq, d), lambda bi, gi, i: (bi, gi, 0, i, 0)),
        out_shape=jax.ShapeDtypeStruct((b, g, r, s, d), MXU_DTYPE),
        input_output_aliases={4: 0},
        compiler_params=_cparams(("arbitrary", "arbitrary", "arbitrary")),
        name="window_attention",
    )(q5, k, v, sink_col, jnp.zeros((b, g, r, s, d), MXU_DTYPE))
    return o.reshape(b, hq, s, d)


def _toeplitz(a, n_q, n_k, off):
    length = a.shape[-1]
    lo = (n_q - 1) - off
    hi = (n_q + n_k - 1) - lo - length
    ap = jnp.pad(a, [(0, 0)] * (a.ndim - 1) + [(max(lo, 0), max(hi, 0))])
    ap = ap[..., max(-lo, 0):ap.shape[-1] - max(-hi, 0)]
    return jnp.stack([ap[..., n_q - 1 - q:n_q - 1 - q + n_k] for q in range(n_q)], axis=-2)


def _na_tables(rpb, rows):
    w = GRID_W
    kh = min(NA_ROWS, rows)
    configs = [(0, 0), (NA_QROWS, NA_QROWS - NA_ROWS // 2), (rows - NA_QROWS, rows - NA_KROWS)]
    qi = jnp.arange(NA_QROWS, dtype=jnp.int32)[:, None, None, None]
    qc = jnp.arange(w, dtype=jnp.int32)[None, :, None, None]
    kj = jnp.arange(NA_KROWS, dtype=jnp.int32)[None, None, :, None]
    kc = jnp.arange(w, dtype=jnp.int32)[None, None, None, :]
    full = (NA_QROWS, w, NA_KROWS, w)
    flat = (NA_QROWS * w, NA_KROWS * w)
    by_col = jnp.moveaxis(_toeplitz(rpb.astype(jnp.float32), w, w, NA_COLS - 1), 1, -1)
    tabs = []
    for r0, kr0 in configs:
        qr = r0 + qi
        kr = kr0 + kj
        rs = jnp.clip(qr - kh // 2, 0, rows - kh)
        cs = jnp.clip(qc - NA_COLS // 2, 0, w - NA_COLS)
        valid = (kr >= rs) & (kr < rs + kh) & (kc >= cs) & (kc < cs + NA_COLS)
        valid = jnp.broadcast_to(valid, full).reshape(flat)
        bias = _toeplitz(by_col, NA_QROWS, NA_KROWS, kr0 - r0 + NA_ROWS - 1)
        bias = jnp.transpose(bias, (0, 3, 1, 4, 2)).reshape((-1,) + flat)
        tabs.append(jnp.where(valid[None], bias, NEG_INF))
    return jnp.stack(tabs, axis=1)


def _na_kernel(q_ref, k_ref, v_ref, tab_ref, init_ref, o_ref, *, tq, tkw, n_lat, n_ctx, nq):
    del init_ref
    i = pl.program_id(2)
    kind = jnp.where(i == 0, 0, jnp.where(i == nq - 1, 2, 1))
    start = pl.multiple_of(jnp.clip(i * tq - (NA_ROWS // 2) * GRID_W, 0, n_lat - tkw), GRID_W)
    k_loc = k_ref[pl.ds(start, tkw), :]
    v_loc = v_ref[pl.ds(start, tkw), :]
    k_ctx = k_ref[n_lat:n_lat + n_ctx, :]
    v_ctx = v_ref[n_lat:n_lat + n_ctx, :]
    parts = [(lo, lo + tq // ATTN_SPLIT) for lo in range(0, tq, tq // ATTN_SPLIT)]
    scores = [(_dot_nt(q_ref[lo:hi, :], k_loc), _dot_nt(q_ref[lo:hi, :], k_ctx)) for lo, hi in parts]
    for (lo, hi), (s_loc, s_ctx) in zip(parts, scores):
        s_loc = s_loc + tab_ref[kind, lo:hi, :]
        m = jnp.maximum(jnp.max(s_loc, axis=-1, keepdims=True), jnp.max(s_ctx, axis=-1, keepdims=True))
        p_loc = jnp.exp(s_loc - m)
        p_ctx = jnp.exp(s_ctx - m)
        denom = jnp.sum(p_loc, axis=-1, keepdims=True) + jnp.sum(p_ctx, axis=-1, keepdims=True)
        o = (_dot(p_loc.astype(MXU_DTYPE), v_loc) + _dot(p_ctx.astype(MXU_DTYPE), v_ctx)) / denom
        o_ref[lo:hi, :] = o.astype(o_ref.dtype)


def _neighbourhood_attention(q, k, v, tables, n_lat, n_ctx):
    b, hq, s, d = q.shape
    tq = NA_QROWS * GRID_W
    tkw = NA_KROWS * GRID_W
    nq = n_lat // tq
    return pl.pallas_call(
        functools.partial(_na_kernel, tq=tq, tkw=tkw, n_lat=n_lat, n_ctx=n_ctx, nq=nq),
        grid=(b, hq, nq),
        in_specs=[
            pl.BlockSpec((None, None, tq, d), lambda bi, h, i: (bi, h, i, 0)),
            pl.BlockSpec((None, None, s, d), lambda bi, h, i: (bi, h, 0, 0)),
            pl.BlockSpec((None, None, s, d), lambda bi, h, i: (bi, h, 0, 0)),
            pl.BlockSpec((None, 3, tq, tkw), lambda bi, h, i: (h, 0, 0, 0)),
            pl.BlockSpec(memory_space=pl.ANY),
        ],
        out_specs=pl.BlockSpec((None, None, tq, d), lambda bi, h, i: (bi, h, i, 0)),
        out_shape=jax.ShapeDtypeStruct((b, hq, s, d), MXU_DTYPE),
        input_output_aliases={4: 0},
        compiler_params=_cparams(("arbitrary", "arbitrary", "arbitrary")),
        name="neighbourhood_attention",
    )(q, k, v, tables, jnp.zeros((b, hq, s, d), MXU_DTYPE))


def _outproj_kernel(o1_ref, o2_ref, w_ref, x_ref, mod_ref, rw_ref, rb_ref,
                    xo_ref, h_ref, lg_ref, *, tm, n_lat, n_heads, transposed):
    row0 = pl.program_id(1) * tm
    if transposed:
        yt = jnp.concatenate([o1_ref[...].reshape(-1, tm), o2_ref[...].reshape(-1, tm)], axis=0)
        z = lax.dot_general(yt, w_ref[...], (((0,), (0,)), ((), ())), preferred_element_type=jnp.float32)
    else:
        y = jnp.concatenate([o1_ref[h] for h in range(n_heads)] + [o2_ref[h] for h in range(n_heads)], axis=-1)
        z = _dot(y, w_ref[...])
    rows = row0 + lax.broadcasted_iota(jnp.int32, (tm, 1), 0)
    gate = jnp.where(rows >= n_lat, mod_ref[6:7, :], mod_ref[4:5, :])
    xn = x_ref[...] + gate * z
    xo_ref[...] = xn
    h = _modulated_norm(xn, mod_ref, row0, n_lat, 0, 1, 2, 3)
    _store_row_tiles(h_ref, h)
    lg_ref[...] = jnp.dot(h, rw_ref[...], preferred_element_type=jnp.float32,
                          precision=lax.Precision.HIGHEST) + rb_ref[...]


def _outproj(o1, o2, w_out, xs, modv, router_w, router_b, n_lat, transposed):
    b, s, dm = xs.shape
    tm = ROW_TILE
    nh = o1.shape[1]
    nl = router_w.shape[1]
    if transposed:
        o_spec = pl.BlockSpec((None, nh, o1.shape[2], tm), lambda bi, j: (bi, 0, 0, j))
    else:
        o_spec = pl.BlockSpec((None, nh, tm, o1.shape[3]), lambda bi, j: (bi, 0, j, 0))
    return pl.pallas_call(
        functools.partial(_outproj_kernel, tm=tm, n_lat=n_lat, n_heads=nh, transposed=transposed),
        grid=(b, s // tm),
        in_specs=[
            o_spec,
            o_spec,
            pl.BlockSpec(w_out.shape, lambda bi, j: (0, 0)),
            pl.BlockSpec((None, tm, dm), lambda bi, j: (bi, j, 0)),
            pl.BlockSpec((None, 8, dm), lambda bi, j: (bi, 0, 0)),
            pl.BlockSpec(router_w.shape, lambda bi, j: (0, 0)),
            pl.BlockSpec(router_b.shape, lambda bi, j: (0, 0)),
        ],
        out_specs=[pl.BlockSpec((None, tm, dm), lambda bi, j: (bi, j, 0)),
                   pl.BlockSpec((None, tm * ROW_TILES, LANES), lambda bi, j: (bi, j, 0)),
                   pl.BlockSpec((None, tm, nl), lambda bi, j: (bi, j, 0))],
        out_shape=[jax.ShapeDtypeStruct((b, s, dm), jnp.float32),
                   jax.ShapeDtypeStruct((b, s * ROW_TILES, LANES), jnp.float32),
                   jax.ShapeDtypeStruct((b, s, nl), jnp.float32)],
        input_output_aliases={3: 0},
        compiler_params=_cparams(("arbitrary", "arbitrary")),
        name="outproj_residual_norm_router",
    )(o1, o2, w_out, xs, modv, router_w, router_b)


def _expert_kernel(be_ref, tok0_ref, tokn_ref, h_hbm, win_ref, bin_ref, wout_ref, bout_ref,
                   y_ref, xbuf, sems, wi_sc, wo_sc, *, blk, nb):
    i = pl.program_id(0)
    slot = lax.rem(i, 2)
    rt = ROW_TILES

    @pl.when(jnp.logical_or(i == 0, be_ref[i] != be_ref[jnp.maximum(i - 1, 0)]))
    def _():
        wi_sc[...] = win_ref[...].astype(wi_sc.dtype)
        wo_sc[...] = wout_ref[...].astype(wo_sc.dtype)

    def gather(tok_ref, dst_slot):
        for rr in range(blk):
            src = pl.multiple_of(tok_ref[0, rr], rt)
            pltpu.make_async_copy(h_hbm.at[pl.ds(src, rt)], xbuf.at[dst_slot, pl.ds(rr * rt, rt)],
                                  sems.at[dst_slot]).start()

    @pl.when(i == 0)
    def _():
        gather(tok0_ref, 0)

    @pl.when(i + 1 < nb)
    def _():
        gather(tokn_ref, 1 - slot)

    pltpu.make_async_copy(h_hbm.at[pl.ds(0, blk * rt)], xbuf.at[slot], sems.at[slot]).wait()

    x = _load_row_tiles(xbuf.at[slot], blk).astype(MXU_DTYPE)
    u = _dot(x, wi_sc[...]) + bin_ref[...]
    glu = jnp.minimum(u[:, :D_EXPERT], SWIGLU_LIMIT)
    lin = jnp.clip(u[:, D_EXPERT:], -SWIGLU_LIMIT, SWIGLU_LIMIT)
    act = glu * jax.nn.sigmoid(SWIGLU_ALPHA * glu) * (lin + 1.0)
    _store_row_tiles(y_ref, _dot(act.astype(MXU_DTYPE), wo_sc[...]) + bout_ref[...])


def _experts(h_tiles, blk_expert, tok_buf, w_in, b_in, w_out, b_out, layer):
    dm = w_in.shape[2]
    blk = MOE_ROWS
    rt = ROW_TILES
    nb = blk_expert.shape[0]
    de2 = w_in.shape[-1]
    tok3 = tok_buf.reshape(nb, 1, blk)
    grid_spec = pltpu.PrefetchScalarGridSpec(
        num_scalar_prefetch=1,
        grid=(nb,),
        in_specs=[
            pl.BlockSpec((None, 1, blk), lambda i, be: (0, 0, 0), memory_space=pltpu.SMEM),
            pl.BlockSpec((None, 1, blk), lambda i, be: (jnp.minimum(i + 1, nb - 1), 0, 0),
                         memory_space=pltpu.SMEM),
            pl.BlockSpec(memory_space=pl.ANY),
            pl.BlockSpec((None, None, dm, de2), lambda i, be: (layer, be[i], 0, 0)),
            pl.BlockSpec((None, 1, de2), lambda i, be: (be[i], 0, 0)),
            pl.BlockSpec((None, None, de2 // 2, dm), lambda i, be: (layer, be[i], 0, 0)),
            pl.BlockSpec((None, 1, dm), lambda i, be: (be[i], 0, 0)),
        ],
        out_specs=pl.BlockSpec((blk * rt, LANES), lambda i, be: (i, 0)),
        scratch_shapes=[pltpu.VMEM((2, blk * rt, LANES), jnp.float32), pltpu.SemaphoreType.DMA((2,)),
                        pltpu.VMEM((dm, de2), MXU_DTYPE), pltpu.VMEM((de2 // 2, dm), MXU_DTYPE)],
    )
    return pl.pallas_call(
        functools.partial(_expert_kernel, blk=blk, nb=nb),
        grid_spec=grid_spec,
        out_shape=jax.ShapeDtypeStruct((nb * blk * rt, LANES), jnp.float32),
        compiler_params=_cparams(("arbitrary",)),
        name="routed_experts",
    )(blk_expert, tok3, tok3, h_tiles, w_in, b_in.reshape(N_EXPERTS, 1, de2), w_out,
      b_out.reshape(N_EXPERTS, 1, dm))


def _combine_kernel(pos0_ref, posn_ref, y_hbm, x_ref, mod_ref, w_ref, o_ref, gbuf, sems, *, tc, nsteps, n_lat,
                    steps_per_batch):
    i = pl.program_id(0)
    slot = lax.rem(i, 2)

    rt = ROW_TILES

    def gather(pos_ref, dst_slot):
        def issue(t, carry):
            for kk in range(TOP_K):
                src = pl.multiple_of(pos_ref[0, t * TOP_K + kk], rt)
                dst = pl.multiple_of(t * rt, rt)
                pltpu.make_async_copy(y_hbm.at[pl.ds(src, rt)], gbuf.at[dst_slot, kk, pl.ds(dst, rt)],
                                      sems.at[dst_slot]).start()
            return carry
        lax.fori_loop(0, tc, issue, 0, unroll=8)

    @pl.when(i == 0)
    def _():
        gather(pos0_ref, 0)

    @pl.when(i + 1 < nsteps)
    def _():
        gather(posn_ref, 1 - slot)

    for kk in range(TOP_K):
        pltpu.make_async_copy(y_hbm.at[pl.ds(0, tc * rt)], gbuf.at[slot, kk], sems.at[slot]).wait()

    w = w_ref[...]
    y = _load_row_tiles(gbuf.at[slot, 0], tc) * w[:, 0:1]
    for kk in range(1, TOP_K):
        y = y + _load_row_tiles(gbuf.at[slot, kk], tc) * w[:, kk:kk + 1]
    row0 = lax.rem(i, steps_per_batch) * tc
    rows = row0 + lax.broadcasted_iota(jnp.int32, (tc, 1), 0)
    gate = jnp.where(rows >= n_lat, mod_ref[7:8, :], mod_ref[5:6, :])
    o_ref[...] = x_ref[...] + gate * y


def _combine(y_sorted, pos, weights, xs_flat, modv, s, n_lat):
    n, dm = xs_flat.shape
    tc = COMBINE_ROWS
    nsteps = n // tc
    spb = s // tc
    pos3 = pos.reshape(nsteps, 1, tc * TOP_K)
    return pl.pallas_call(
        functools.partial(_combine_kernel, tc=tc, nsteps=nsteps, n_lat=n_lat, steps_per_batch=spb),
        grid=(nsteps,),
        in_specs=[
            pl.BlockSpec((None, 1, tc * TOP_K), lambda i: (0, 0, 0), memory_space=pltpu.SMEM),
            pl.BlockSpec((None, 1, tc * TOP_K), lambda i: (jnp.minimum(i + 1, nsteps - 1), 0, 0),
                         memory_space=pltpu.SMEM),
            pl.BlockSpec(memory_space=pl.ANY),
            pl.BlockSpec((tc, dm), lambda i: (i, 0)),
            pl.BlockSpec((None, 8, dm), lambda i: (i // spb, 0, 0)),
            pl.BlockSpec((tc, TOP_K), lambda i: (i, 0)),
        ],
        out_specs=pl.BlockSpec((tc, dm), lambda i: (i, 0)),
        out_shape=jax.ShapeDtypeStruct((n, dm), jnp.float32),
        scratch_shapes=[pltpu.VMEM((2, TOP_K, tc * ROW_TILES, LANES), jnp.float32), pltpu.SemaphoreType.DMA((2,))],
        input_output_aliases={3: 0},
        compiler_params=_cparams(("arbitrary",)),
        name="expert_combine",
    )(pos3, pos3, y_sorted, xs_flat, modv, weights)


def _final_norm_kernel(x_ref, g_ref, o_ref):
    x = x_ref[...]
    o_ref[...] = x * lax.rsqrt(jnp.mean(x * x, axis=-1, keepdims=True) + NORM_EPS) * g_ref[...]


def _final_norm(xs, g, n_lat):
    b, s, dm = xs.shape
    tm = 512
    return pl.pallas_call(
        _final_norm_kernel,
        grid=(b, n_lat // tm),
        in_specs=[pl.BlockSpec((None, tm, dm), lambda bi, j: (bi, j, 0)),
                  pl.BlockSpec((1, dm), lambda bi, j: (0, 0))],
        out_specs=pl.BlockSpec((None, tm, dm), lambda bi, j: (bi, j, 0)),
        out_shape=jax.ShapeDtypeStruct((b, n_lat, dm), jnp.float32),
        compiler_params=_cparams(("arbitrary", "arbitrary")),
        name="final_norm",
    )(xs, g.reshape(1, dm))


def _swap_perm(width, dim):
    j = np.arange(width)
    half = dim // 2
    return np.where((j % dim) < half, j + half, j - half)


def _rope_tables(n_lat, n_ctx, dim):
    t = jnp.arange(n_lat, dtype=jnp.int32)
    row = (t // GRID_W).astype(jnp.float32)
    col = (t % GRID_W).astype(jnp.float32)
    quarter = dim // 4
    inv_freq = ROPE_BASE ** (-jnp.arange(quarter, dtype=jnp.float32) / quarter)
    ang = jnp.concatenate([row[:, None] * inv_freq, col[:, None] * inv_freq], axis=-1)
    cos, sin = jnp.cos(ang), jnp.sin(ang)
    reps = LANES // dim
    cos_t = jnp.tile(jnp.concatenate([cos, cos], axis=-1), (1, reps))
    sin_t = jnp.tile(jnp.concatenate([-sin, sin], axis=-1), (1, reps))
    cos_t = jnp.concatenate([cos_t, jnp.ones((n_ctx, LANES), jnp.float32)], axis=0)
    sin_t = jnp.concatenate([sin_t, jnp.zeros((n_ctx, LANES), jnp.float32)], axis=0)
    return cos_t, sin_t


def _mod_rows(mod_l, norm_g, b, first):
    lat, ctx = mod_l[:b], mod_l[b]
    a_lat = norm_g[None] * (1.0 + lat[:, first + 1])
    b_lat = lat[:, first]
    a_ctx = jnp.broadcast_to(norm_g * (1.0 + ctx[first + 1]), a_lat.shape)
    b_ctx = jnp.broadcast_to(ctx[first], a_lat.shape)
    g_mix_ctx = jnp.broadcast_to(ctx[2], a_lat.shape)
    g_ffn_ctx = jnp.broadcast_to(ctx[5], a_lat.shape)
    return jnp.stack([a_lat, b_lat, a_ctx, b_ctx, lat[:, 2], lat[:, 5], g_mix_ctx, g_ffn_ctx], axis=1)


def _route(logits, blk):
    n = logits.shape[0]
    nk = n * TOP_K
    top_val, top_idx = lax.top_k(logits, TOP_K)
    gate = jax.nn.softmax(top_val, axis=-1)
    expert = top_idx.reshape(-1)
    onehot = (expert[:, None] == jnp.arange(N_EXPERTS, dtype=jnp.int32)[None, :]).astype(jnp.int32)
    csum = jnp.cumsum(onehot, axis=0)
    rank = jnp.take_along_axis(csum, expert[:, None], axis=1)[:, 0] - 1
    sizes = csum[-1]
    padded = (sizes + blk - 1) // blk * blk
    pends = jnp.cumsum(padded)
    pstarts = pends - padded
    dest = pstarts[expert] + rank
    nb = -(-(nk + N_EXPERTS * (blk - 1)) // blk)
    blk_expert = jnp.minimum(jnp.searchsorted(pends, jnp.arange(nb, dtype=jnp.int32) * blk, side='right'),
                             N_EXPERTS - 1).astype(jnp.int32)
    order = jnp.argsort(expert)
    starts = jnp.cumsum(sizes) - sizes
    slot = jnp.arange(nb * blk, dtype=jnp.int32)
    slot_expert = jnp.repeat(blk_expert, blk)
    slot_rank = slot - pstarts[slot_expert]
    src = jnp.minimum(starts[slot_expert] + slot_rank, nk - 1)
    tok_buf = jnp.where(slot_rank < sizes[slot_expert], order[src] // TOP_K, 0).astype(jnp.int32)
    return blk_expert, tok_buf, gate, dest.astype(jnp.int32)


def kernel(x, c, ctx, c_ctx, mod_w, mod_b, norm_mix, norm_ffn, ab_w_in, ab_w_out, a_sink, b_rpb,
           cd_w_in, c_q_norm, c_w_q_b, c_kv_norm, c_w_kv_b, d_q_norm, d_k_norm, cd_w_out,
           router_w, router_b, exp_w_in, exp_b_in, exp_w_out, exp_b_out, final_norm):
    b, t, dm = x.shape
    assert dm == ROW_TILES * LANES
    n_ctx = ctx.shape[1]
    s = t + n_ctx
    depth = mod_w.shape[0]
    rows = t // GRID_W
    d = HEAD_DIM
    f32 = jnp.float32

    xs = jnp.concatenate([x, ctx], axis=1)

    cond = jnp.zeros((8, dm), f32).at[:b].set(c).at[b].set(c_ctx)
    mod_all = _modulation(cond, mod_w, mod_b)[:, :b + 1].reshape(depth, b + 1, 6, dm)

    cos_t, sin_t = _rope_tables(t, n_ctx, d)
    cosm_t, sinm_t = _rope_tables(t, n_ctx, C_ROPE)

    router_w_p = jnp.zeros((depth, dm, LANES), f32).at[:, :, :N_EXPERTS].set(router_w)
    router_b_p = jnp.zeros((depth, 1, LANES), f32).at[:, 0, :N_EXPERTS].set(router_b)

    p64_512 = _swap_perm(512, d)
    p64_128 = _swap_perm(128, d)
    p32 = _swap_perm(C_ROPE, C_ROPE)

    for layer in range(depth):
        i = layer // 2
        modv_mix = _mod_rows(mod_all[layer], norm_mix[layer], b, 0)
        modv_ffn = _mod_rows(mod_all[layer], norm_ffn[layer], b, 3)
        if layer % 2 == 0:
            w = ab_w_in[i]
            w_ext = jnp.concatenate([w, w[:, 0:512][:, p64_512], w[:, 512:640][:, p64_128]], axis=1).astype(MXU_DTYPE)
            qa, ka, va, qb, kb, vb = _proj_ab(xs, modv_mix, w_ext, cos_t, sin_t, t)
            sink_gr = a_sink[i].astype(f32).reshape(A_KV_HEADS, A_HEADS // A_KV_HEADS)
            r = A_HEADS // A_KV_HEADS
            sink_win = jnp.repeat(sink_gr, WIN_Q, axis=1).reshape(A_KV_HEADS, r * WIN_Q, 1)
            sink_ctx = jnp.repeat(sink_gr, n_ctx, axis=1).reshape(A_KV_HEADS, r * n_ctx, 1)
            o1 = _window_attention(qa, ka, va, sink_win, t, n_ctx)
            o1 = _flash(qa, ka, va, groups=A_KV_HEADS, tq=n_ctx, tk=n_ctx, q_row0=t, n_q=n_ctx,
                        kv_row0=t, n_kv=n_ctx, sink=sink_ctx, out=o1)
            tables = _na_tables(b_rpb[i], rows)
            o2 = _neighbourhood_attention(qb, kb, vb, tables, t, n_ctx)
            o2 = _flash(qb, kb, vb, groups=B_HEADS, tq=n_ctx, tk=n_ctx, q_row0=t, n_q=n_ctx,
                        kv_row0=t, n_kv=n_ctx, out=o2)
            w_out = ab_w_out[i].astype(MXU_DTYPE)
        else:
            w = cd_w_in[i]
            base = C_Q_RANK + C_KV_RANK
            o_qd, o_kd, o_vd = base + C_ROPE, base + C_ROPE + 512, base + C_ROPE + 640
            wqd, wkd = w[:, o_qd:o_qd + 512], w[:, o_kd:o_kd + 128]
            gq = jnp.tile(d_q_norm[i].astype(f32), D_HEADS)
            gk = jnp.tile(d_k_norm[i].astype(f32), D_KV_HEADS)
            krope = w[:, base:base + C_ROPE]
            pad = jnp.zeros((dm, LANES - C_ROPE), f32)
            w1 = jnp.concatenate([
                w[:, 0:base], wqd, wkd, w[:, o_vd:o_vd + 128],
                (wqd * gq[None])[:, p64_512], (wkd * gk[None])[:, p64_128],
                krope, pad, krope[:, p32], pad], axis=1).astype(MXU_DTYPE)
            dq = C_NOPE + C_ROPE
            wq = c_w_q_b[i].reshape(C_Q_RANK, C_HEADS, dq)
            wq_nope = wq[:, :, :C_NOPE].reshape(C_Q_RANK, C_HEADS * C_NOPE)
            wq_rope = wq[:, :, C_NOPE:].reshape(C_Q_RANK, C_HEADS * C_ROPE)
            wq2 = jnp.concatenate([wq_nope, wq_rope, wq_rope[:, _swap_perm(C_HEADS * C_ROPE, C_ROPE)]],
                                  axis=1).astype(MXU_DTYPE)
            wkv = c_w_kv_b[i].reshape(C_KV_RANK, C_HEADS, C_NOPE + C_V)
            wkv2 = jnp.concatenate([wkv[:, :, :C_NOPE].reshape(C_KV_RANK, -1),
                                    wkv[:, :, C_NOPE:].reshape(C_KV_RANK, -1)], axis=1).astype(MXU_DTYPE)
            qc, kc, vc, qd, kd, vd = _proj_cd(
                xs, modv_mix, w1, wq2, wkv2, c_q_norm[i].astype(f32).reshape(1, -1),
                c_kv_norm[i].astype(f32).reshape(1, -1), gq.reshape(1, -1), gk.reshape(1, -1),
                cos_t, sin_t, cosm_t, sinm_t, t)
            o1 = _flash_t(qc, kc, vc, groups=C_HEADS, tq=FLASH_M, tk=FLASH_TK, q_row0=0, n_q=t, kv_row0=0, n_kv=s,
                          out=jnp.zeros((b, C_HEADS, C_V, s), MXU_DTYPE))
            o1 = _flash_t(qc, kc, vc, groups=C_HEADS, tq=n_ctx, tk=n_ctx, q_row0=t, n_q=n_ctx,
                          kv_row0=t, n_kv=n_ctx, out=o1)
            rd = D_HEADS // D_KV_HEADS
            o2 = _flash_t(qd, kd, vd, groups=D_KV_HEADS, tq=FLASH_M // rd, tk=FLASH_TK, q_row0=0, n_q=t,
                          kv_row0=0, n_kv=s, out=jnp.zeros((b, D_HEADS, d, s), MXU_DTYPE))
            o2 = _flash_t(qd, kd, vd, groups=D_KV_HEADS, tq=n_ctx, tk=n_ctx, q_row0=t, n_q=n_ctx,
                          kv_row0=t, n_kv=n_ctx, out=o2)
            w_out = cd_w_out[i].astype(MXU_DTYPE)

        xs, h2, logits = _outproj(o1, o2, w_out, xs, modv_ffn, router_w_p[layer], router_b_p[layer], t,
                                  transposed=layer % 2 == 1)

        n = b * s
        blk_expert, tok_buf, weights, pos = _route(logits.reshape(n, LANES)[:, :N_EXPERTS], MOE_ROWS)
        y_sorted = _experts(h2.reshape(n * ROW_TILES, LANES), blk_expert, tok_buf * ROW_TILES,
                            exp_w_in.astype(f32), exp_b_in[layer].astype(f32),
                            exp_w_out.astype(f32), exp_b_out[layer].astype(f32), layer)
        xs = _combine(y_sorted, pos * ROW_TILES, weights, xs.reshape(n, dm), modv_ffn, s, t).reshape(b, s, dm)

    return _final_norm(xs, final_norm.astype(f32), t)
```

```python
import functools

import jax
import jax.numpy as jnp
import numpy as np
from jax import lax
from jax.experimental import pallas as pl
from jax.experimental.pallas import tpu as pltpu

GRID_W = 64
HEAD_DIM = 64
ROPE_BASE = 10000.0
NORM_EPS = 1e-6
NEG_INF = -1e30
A_HEADS = 8
A_KV_HEADS = 2
A_WINDOW = 128
B_HEADS = 8
NA_ROWS = 8
NA_COLS = 16
C_HEADS = 8
C_Q_RANK = 768
C_KV_RANK = 256
C_NOPE = 64
C_ROPE = 32
C_V = 64
D_HEADS = 8
D_KV_HEADS = 2
N_EXPERTS = 32
TOP_K = 4
D_EXPERT = 1024
SWIGLU_LIMIT = 7.0
SWIGLU_ALPHA = 1.702
LOG2_E = 1.4426950408889634

LANES = 128
BF16_SUBLANES = 16
ROW_TILES = 8
VMEM_LIMIT_BYTES = 56 * 1024 * 1024

MXU_DTYPE = jnp.bfloat16

ROW_TILE = 640
MOE_ROWS = 256
COMBINE_ROWS = 256
NA_QROWS = 8
NA_KROWS = 16
WIN_Q = 256
ATTN_SPLIT = 2
FLASH_M = 1024
FLASH_TK = 640
FLASH_UNROLL = 8


def _cparams(sem, vmem=VMEM_LIMIT_BYTES):
    return pltpu.CompilerParams(dimension_semantics=sem, vmem_limit_bytes=vmem)


def _dot(a, b):
    return jnp.dot(a, b, preferred_element_type=jnp.float32)


def _dot_nt(a, b):
    return lax.dot_general(a, b, (((1,), (1,)), ((), ())), preferred_element_type=jnp.float32)


def _mod_kernel(c_ref, w_ref, b_ref, o_ref):
    c = c_ref[...]
    s = c * jax.nn.sigmoid(c)
    o_ref[...] = jnp.dot(s, w_ref[...], preferred_element_type=jnp.float32,
                         precision=lax.Precision.HIGHEST) + b_ref[...]


def _modulation(cond, mod_w, mod_b):
    depth, dm, n = mod_w.shape
    tn = 1536
    return pl.pallas_call(
        _mod_kernel,
        grid=(depth, n // tn),
        in_specs=[
            pl.BlockSpec((8, dm), lambda l, j: (0, 0)),
            pl.BlockSpec((None, dm, tn), lambda l, j: (l, 0, j)),
            pl.BlockSpec((None, 1, tn), lambda l, j: (l, 0, j)),
        ],
        out_specs=pl.BlockSpec((None, 8, tn), lambda l, j: (l, 0, j)),
        out_shape=jax.ShapeDtypeStruct((depth, 8, n), jnp.float32),
        compiler_params=_cparams(("arbitrary", "arbitrary")),
        name="adaln_modulation",
    )(cond, mod_w, mod_b.reshape(depth, 1, n))


def _modulated_norm(x, mod_ref, row0, n_lat, a_lat, b_lat, a_ctx, b_ctx):
    tm = x.shape[0]
    ms = jnp.mean(x * x, axis=-1, keepdims=True)
    xh = x * lax.rsqrt(ms + NORM_EPS)
    rows = row0 + lax.broadcasted_iota(jnp.int32, (tm, 1), 0)
    is_ctx = rows >= n_lat
    a = jnp.where(is_ctx, mod_ref[a_ctx:a_ctx + 1, :], mod_ref[a_lat:a_lat + 1, :])
    b = jnp.where(is_ctx, mod_ref[b_ctx:b_ctx + 1, :], mod_ref[b_lat:b_lat + 1, :])
    return xh * a + b


def _store_heads(o_ref, val, n_heads, width, lane0=0):
    for h in range(n_heads):
        o_ref[h, :, lane0:lane0 + width] = val[:, h * width:(h + 1) * width].astype(o_ref.dtype)


def _store_row_tiles(o_ref, val):
    rows = val.shape[0]
    for c in range(ROW_TILES):
        o_ref[pl.ds(c, rows, stride=ROW_TILES), :] = val[:, c * LANES:(c + 1) * LANES]


def _load_row_tiles(ref, rows):
    return jnp.concatenate([ref[pl.ds(c, rows, stride=ROW_TILES), :] for c in range(ROW_TILES)], axis=-1)


def _tile_lanes(t, reps):
    return t if reps == 1 else jnp.concatenate([t] * reps, axis=-1)


def _proj_ab_kernel(x_ref, mod_ref, w_ref, cos_ref, sin_ref,
                    qa_ref, ka_ref, va_ref, qb_ref, kb_ref, vb_ref, *, tm, n_lat):
    row0 = pl.program_id(1) * tm
    h = _modulated_norm(x_ref[...], mod_ref, row0, n_lat, 0, 1, 2, 3).astype(MXU_DTYPE)
    cos = cos_ref[...]
    sin = sin_ref[...]
    d = HEAD_DIM
    scale = d ** -0.5
    qa = _dot(h, w_ref[:, 0:512]) * _tile_lanes(cos, 4) + _dot(h, w_ref[:, 2304:2816]) * _tile_lanes(sin, 4)
    _store_heads(qa_ref, qa * scale, A_HEADS, d)
    ka = _dot(h, w_ref[:, 512:640]) * cos + _dot(h, w_ref[:, 2816:2944]) * sin
    _store_heads(ka_ref, ka, A_KV_HEADS, d)
    _store_heads(va_ref, _dot(h, w_ref[:, 640:768]), A_KV_HEADS, d)
    _store_heads(qb_ref, _dot(h, w_ref[:, 768:1280]) * scale, B_HEADS, d)
    _store_heads(kb_ref, _dot(h, w_ref[:, 1280:1792]), B_HEADS, d)
    _store_heads(vb_ref, _dot(h, w_ref[:, 1792:2304]), B_HEADS, d)


def _proj_ab(xs, modv, w_ext, cos_t, sin_t, n_lat):
    b, s, dm = xs.shape
    tm = ROW_TILE
    d = HEAD_DIM
    nw = w_ext.shape[1]

    def head_spec(nh):
        return pl.BlockSpec((None, nh, tm, d), lambda bi, j: (bi, 0, j, 0))

    def head_shape(nh):
        return jax.ShapeDtypeStruct((b, nh, s, d), MXU_DTYPE)

    return pl.pallas_call(
        functools.partial(_proj_ab_kernel, tm=tm, n_lat=n_lat),
        grid=(b, s // tm),
        in_specs=[
            pl.BlockSpec((None, tm, dm), lambda bi, j: (bi, j, 0)),
            pl.BlockSpec((None, 8, dm), lambda bi, j: (bi, 0, 0)),
            pl.BlockSpec((dm, nw), lambda bi, j: (0, 0)),
            pl.BlockSpec((tm, LANES), lambda bi, j: (j, 0)),
            pl.BlockSpec((tm, LANES), lambda bi, j: (j, 0)),
        ],
        out_specs=[head_spec(A_HEADS), head_spec(A_KV_HEADS), head_spec(A_KV_HEADS),
                   head_spec(B_HEADS), head_spec(B_HEADS), head_spec(B_HEADS)],
        out_shape=[head_shape(A_HEADS), head_shape(A_KV_HEADS), head_shape(A_KV_HEADS),
                   head_shape(B_HEADS), head_shape(B_HEADS), head_shape(B_HEADS)],
        compiler_params=_cparams(("arbitrary", "arbitrary")),
        name="proj_ab",
    )(xs, modv, w_ext, cos_t, sin_t)


def _proj_cd_kernel(x_ref, mod_ref, w1_ref, wq2_ref, wkv2_ref, qn_ref, kvn_ref, gq_ref, gk_ref,
                    cos_ref, sin_ref, cosm_ref, sinm_ref,
                    qc_ref, kc_ref, vc_ref, qd_ref, kd_ref, vd_ref, *, tm, n_lat):
    row0 = pl.program_id(1) * tm
    h = _modulated_norm(x_ref[...], mod_ref, row0, n_lat, 0, 1, 2, 3).astype(MXU_DTYPE)
    cos = cos_ref[...]
    sin = sin_ref[...]
    cosm = cosm_ref[...]
    sinm = sinm_ref[...]
    d = HEAD_DIM

    cq = _dot(h, w1_ref[:, 0:768])
    ckv = _dot(h, w1_ref[:, 768:1024])
    qd_raw, qd_swp = _dot(h, w1_ref[:, 1024:1536]), _dot(h, w1_ref[:, 1792:2304])
    kd_raw, kd_swp = _dot(h, w1_ref[:, 1536:1664]), _dot(h, w1_ref[:, 2304:2432])
    vd = _dot(h, w1_ref[:, 1664:1792])
    kr_raw, kr_swp = _dot(h, w1_ref[:, 2432:2560]), _dot(h, w1_ref[:, 2560:2688])

    cqn = cq * lax.rsqrt(jnp.mean(cq * cq, axis=-1, keepdims=True) + NORM_EPS) * qn_ref[...]
    ckvn = ckv * lax.rsqrt(jnp.mean(ckv * ckv, axis=-1, keepdims=True) + NORM_EPS) * kvn_ref[...]
    q2 = _dot(cqn.astype(MXU_DTYPE), wq2_ref[...])
    kv2 = _dot(ckvn.astype(MXU_DTYPE), wkv2_ref[...])

    def normed_rope(raw, swapped_gained, gain, n_heads, o_ref, scale):
        reps = n_heads * d // LANES
        val = raw * (_tile_lanes(cos, reps) * gain) + swapped_gained * _tile_lanes(sin, reps)
        for hh in range(n_heads):
            sl = raw[:, hh * d:(hh + 1) * d]
            r = lax.rsqrt(jnp.mean(sl * sl, axis=-1, keepdims=True) + NORM_EPS)
            o_ref[hh, :, :] = (val[:, hh * d:(hh + 1) * d] * (r * scale)).astype(o_ref.dtype)

    normed_rope(qd_raw, qd_swp, gq_ref[...], D_HEADS, qd_ref, d ** -0.5 * LOG2_E)
    normed_rope(kd_raw, kd_swp, gk_ref[...], D_KV_HEADS, kd_ref, 1.0)
    _store_heads(vd_ref, vd, D_KV_HEADS, d)

    c_scale = (C_NOPE + C_ROPE) ** -0.5 * LOG2_E
    q_rope = (q2[:, 512:768] * _tile_lanes(cosm, 2) + q2[:, 768:1024] * _tile_lanes(sinm, 2)) * c_scale
    _store_heads(qc_ref, q2[:, 0:512] * c_scale, C_HEADS, C_NOPE)
    _store_heads(qc_ref, q_rope, C_HEADS, C_ROPE, lane0=C_NOPE)
    k_rope = (kr_raw * cosm + kr_swp * sinm)[:, 0:C_ROPE]
    _store_heads(kc_ref, kv2[:, 0:512], C_HEADS, C_NOPE)
    for hh in range(C_HEADS):
        kc_ref[hh, :, C_NOPE:C_NOPE + C_ROPE] = k_rope.astype(kc_ref.dtype)
    _store_heads(vc_ref, kv2[:, 512:1024], C_HEADS, C_V)


def _proj_cd(xs, modv, w1, wq2, wkv2, qn, kvn, gq, gk, cos_t, sin_t, cosm_t, sinm_t, n_lat):
    b, s, dm = xs.shape
    tm = ROW_TILE
    dk_c = C_NOPE + C_ROPE

    def full(a):
        return pl.BlockSpec(a.shape, lambda bi, j: (0,) * a.ndim)

    def head_spec(nh, w):
        return pl.BlockSpec((None, nh, tm, w), lambda bi, j: (bi, 0, j, 0))

    def head_shape(nh, w):
        return jax.ShapeDtypeStruct((b, nh, s, w), MXU_DTYPE)

    tab = pl.BlockSpec((tm, LANES), lambda bi, j: (j, 0))
    return pl.pallas_call(
        functools.partial(_proj_cd_kernel, tm=tm, n_lat=n_lat),
        grid=(b, s // tm),
        in_specs=[
            pl.BlockSpec((None, tm, dm), lambda bi, j: (bi, j, 0)),
            pl.BlockSpec((None, 8, dm), lambda bi, j: (bi, 0, 0)),
            full(w1), full(wq2), full(wkv2), full(qn), full(kvn), full(gq), full(gk),
            tab, tab, tab, tab,
        ],
        out_specs=[head_spec(C_HEADS, dk_c), head_spec(C_HEADS, dk_c), head_spec(C_HEADS, C_V),
                   head_spec(D_HEADS, HEAD_DIM), head_spec(D_KV_HEADS, HEAD_DIM), head_spec(D_KV_HEADS, HEAD_DIM)],
        out_shape=[head_shape(C_HEADS, dk_c), head_shape(C_HEADS, dk_c), head_shape(C_HEADS, C_V),
                   head_shape(D_HEADS, HEAD_DIM), head_shape(D_KV_HEADS, HEAD_DIM), head_shape(D_KV_HEADS, HEAD_DIM)],
        compiler_params=_cparams(("arbitrary", "arbitrary")),
        name="proj_cd",
    )(xs, modv, w1, wq2, wkv2, qn, kvn, gq, gk, cos_t, sin_t, cosm_t, sinm_t)


def _flash_kernel(*refs, r, tq, tk, nk, has_sink):
    if has_sink:
        q_ref, k_ref, v_ref, sink_ref, o_ref, m_sc, l_sc, acc_sc = refs
    else:
        q_ref, k_ref, v_ref, o_ref, m_sc, l_sc, acc_sc = refs
    m_rows = r * tq
    q = q_ref[...].reshape(m_rows, q_ref.shape[-1])
    if has_sink:
        m_sc[...] = sink_ref[...]
        l_sc[...] = jnp.ones_like(l_sc)
    else:
        m_sc[...] = jnp.full_like(m_sc, NEG_INF)
        l_sc[...] = jnp.zeros_like(l_sc)
    acc_sc[...] = jnp.zeros_like(acc_sc)

    def body(j, carry):
        k0 = pl.multiple_of(j * tk, tk)
        k = k_ref[pl.ds(k0, tk), :]
        v = v_ref[pl.ds(k0, tk), :]
        s = _dot_nt(q, k)
        m_prev = m_sc[...]
        m_new = jnp.maximum(m_prev, jnp.max(s, axis=-1, keepdims=True))
        alpha = jnp.exp(m_prev - m_new)
        p = jnp.exp(s - m_new)
        l_sc[...] = alpha * l_sc[...] + jnp.sum(p, axis=-1, keepdims=True)
        acc_sc[...] = alpha * acc_sc[...] + _dot(p.astype(MXU_DTYPE), v)
        m_sc[...] = m_new
        return carry

    lax.fori_loop(0, nk, body, 0)
    o = acc_sc[...] / l_sc[...]
    o_ref[...] = o.reshape(o_ref.shape).astype(o_ref.dtype)


def _flash(q, k, v, *, groups, tq, tk, q_row0, n_q, kv_row0, n_kv, sink=None, out=None):
    b, hq, s, dk = q.shape
    dv = v.shape[-1]
    r = hq // groups
    q5 = q.reshape(b, groups, r, s, dk)
    nk = n_kv // tk
    has_sink = sink is not None
    m_rows = r * tq
    in_specs = [
        pl.BlockSpec((None, None, r, tq, dk), lambda bi, g, i: (bi, g, 0, q_row0 // tq + i, 0)),
        pl.BlockSpec((None, None, n_kv, dk), lambda bi, g, i: (bi, g, kv_row0 // n_kv, 0)),
        pl.BlockSpec((None, None, n_kv, dv), lambda bi, g, i: (bi, g, kv_row0 // n_kv, 0)),
    ]
    args = [q5, k, v]
    if has_sink:
        in_specs.append(pl.BlockSpec((None, m_rows, 1), lambda bi, g, i: (g, 0, 0)))
        args.append(sink)
    aliases = {}
    if out is not None:
        in_specs.append(pl.BlockSpec(memory_space=pl.ANY))
        args.append(out.reshape(b, groups, r, s, dv))
        aliases = {len(args) - 1: 0}
    kern = functools.partial(_flash_kernel, r=r, tq=tq, tk=tk, nk=nk, has_sink=has_sink)
    if out is not None:
        inner = kern
        kern = lambda *refs: inner(*refs[:len(args) - 1], *refs[len(args):])
    o = pl.pallas_call(
        kern,
        grid=(b, groups, n_q // tq),
        in_specs=in_specs,
        out_specs=pl.BlockSpec((None, None, r, tq, dv), lambda bi, g, i: (bi, g, 0, q_row0 // tq + i, 0)),
        out_shape=jax.ShapeDtypeStruct((b, groups, r, s, dv), MXU_DTYPE),
        scratch_shapes=[pltpu.VMEM((m_rows, 1), jnp.float32), pltpu.VMEM((m_rows, 1), jnp.float32),
                        pltpu.VMEM((m_rows, dv), jnp.float32)],
        input_output_aliases=aliases,
        compiler_params=_cparams(("arbitrary", "arbitrary", "arbitrary")),
        name="dense_attention",
    )(*args)
    return o.reshape(b, hq, s, dv)


def _flash_t_kernel(q_ref, k_ref, vt_ref, init_ref, o_ref, m_sc, acc_sc, sa_sc, sb_sc, *, r, tq, tk, nk, dv):
    del init_ref
    m_cols = r * tq
    q = q_ref[...].reshape(m_cols, q_ref.shape[-1])
    m_sc[...] = jnp.full_like(m_sc, NEG_INF)
    acc_sc[...] = jnp.zeros_like(acc_sc)
    bufs = (sa_sc, sb_sc)

    def scores(j, s_ref):
        k0 = pl.multiple_of(j * tk, tk)
        s_ref[...] = _dot_nt(k_ref[pl.ds(k0, tk), :], q)

    def update(j, s_ref):
        st = s_ref[...]
        m_prev = m_sc[...]
        m_new = jnp.maximum(m_prev, jnp.max(st, axis=0, keepdims=True))
        alpha = jnp.exp2(m_prev - m_new)
        pt = jnp.exp2(st - m_new)
        acc_sc[...] = alpha * acc_sc[...] + _dot(vt_ref[j], pt.astype(MXU_DTYPE))
        m_sc[...] = m_new

    scores(0, sa_sc)
    trips = (nk - 1) // FLASH_UNROLL

    def body(jj, carry):
        j = FLASH_UNROLL * jj
        for c in range(FLASH_UNROLL):
            scores(j + c + 1, bufs[(c + 1) % 2])
            update(j + c, bufs[c % 2])
        return carry

    lax.fori_loop(0, trips, body, 0)
    for j in range(trips * FLASH_UNROLL, nk):
        if j + 1 < nk:
            scores(j + 1, bufs[(j + 1) % 2])
        update(j, bufs[j % 2])
    o = acc_sc[0:dv, :] / acc_sc[dv:dv + 1, :]
    for rr in range(r):
        o_ref[rr] = o[:, rr * tq:(rr + 1) * tq].astype(o_ref.dtype)


def _flash_t(q, k, v, *, groups, tq, tk, q_row0, n_q, kv_row0, n_kv, out):
    b, hq, s, dk = q.shape
    dv = v.shape[-1]
    r = hq // groups
    q5 = q.reshape(b, groups, r, s, dk)
    nk = n_kv // tk
    vt = jnp.swapaxes(lax.slice_in_dim(v, kv_row0, kv_row0 + n_kv, axis=2).reshape(b, groups, nk, tk, dv), 3, 4)
    dva = dv + BF16_SUBLANES
    vt = jnp.concatenate([vt, jnp.ones((b, groups, nk, BF16_SUBLANES, tk), vt.dtype)], axis=3)
    m_cols = r * tq
    o = pl.pallas_call(
        functools.partial(_flash_t_kernel, r=r, tq=tq, tk=tk, nk=nk, dv=dv),
        grid=(b, groups, n_q // tq),
        in_specs=[
            pl.BlockSpec((None, None, r, tq, dk), lambda bi, g, i: (bi, g, 0, q_row0 // tq + i, 0)),
            pl.BlockSpec((None, None, n_kv, dk), lambda bi, g, i: (bi, g, kv_row0 // n_kv, 0)),
            pl.BlockSpec((None, None, nk, dva, tk), lambda bi, g, i: (bi, g, 0, 0, 0)),
            pl.BlockSpec(memory_space=pl.ANY),
        ],
        out_specs=pl.BlockSpec((None, None, r, dv, tq), lambda bi, g, i: (bi, g, 0, 0, q_row0 // tq + i)),
        out_shape=jax.ShapeDtypeStruct((b, groups, r, dv, s), MXU_DTYPE),
        scratch_shapes=[pltpu.VMEM((1, m_cols), jnp.float32),
                        pltpu.VMEM((dva, m_cols), jnp.float32),
                        pltpu.VMEM((tk, m_cols), jnp.float32), pltpu.VMEM((tk, m_cols), jnp.float32)],
        input_output_aliases={3: 0},
        compiler_params=_cparams(("arbitrary", "arbitrary", "arbitrary")),
        name="dense_attention_t",
    )(q5, k, vt, out.reshape(b, groups, r, dv, s))
    return o.reshape(b, hq, dv, s)


def _window_kernel(q_ref, k_ref, v_ref, sink_ref, init_ref, o_ref, *, r, tq, n_lat, n_ctx):
    del init_ref
    i = pl.program_id(2)
    span = tq + 2 * A_WINDOW
    d = q_ref.shape[-1]
    start = pl.multiple_of(jnp.clip(i * tq - A_WINDOW, 0, n_lat - span), A_WINDOW)
    k_loc = k_ref[pl.ds(start, span), :]
    v_loc = v_ref[pl.ds(start, span), :]
    k_ctx = k_ref[n_lat:n_lat + n_ctx, :]
    v_ctx = v_ref[n_lat:n_lat + n_ctx, :]
    hp = r // ATTN_SPLIT
    rows = hp * tq
    qs = [q_ref[p * hp:(p + 1) * hp].reshape(rows, d) for p in range(ATTN_SPLIT)]
    scores = [(_dot_nt(qp, k_loc), _dot_nt(qp, k_ctx)) for qp in qs]
    qpos = i * tq + lax.rem(lax.broadcasted_iota(jnp.int32, (rows, span), 0), tq)
    kpos = start + lax.broadcasted_iota(jnp.int32, (rows, span), 1)
    ok = jnp.abs(qpos - kpos) <= A_WINDOW
    for p, (s_loc, s_ctx) in enumerate(scores):
        s_loc = jnp.where(ok, s_loc, NEG_INF)
        sink = sink_ref[p * rows:(p + 1) * rows, :]
        m = jnp.maximum(jnp.maximum(jnp.max(s_loc, axis=-1, keepdims=True),
                                    jnp.max(s_ctx, axis=-1, keepdims=True)), sink)
        p_loc = jnp.exp(s_loc - m)
        p_ctx = jnp.exp(s_ctx - m)
        denom = (jnp.sum(p_loc, axis=-1, keepdims=True) + jnp.sum(p_ctx, axis=-1, keepdims=True)
                 + jnp.exp(sink - m))
        o = (_dot(p_loc.astype(MXU_DTYPE), v_loc) + _dot(p_ctx.astype(MXU_DTYPE), v_ctx)) / denom
        o_ref[p * hp:(p + 1) * hp] = o.reshape(hp, tq, d).astype(o_ref.dtype)


def _window_attention(q, k, v, sink_col, n_lat, n_ctx):
    b, hq, s, d = q.shape
    g = k.shape[1]
    r = hq // g
    tq = WIN_Q
    q5 = q.reshape(b, g, r, s, d)
    o = pl.pallas_call(
        functools.partial(_window_kernel, r=r, tq=tq, n_lat=n_lat, n_ctx=n_ctx),
        grid=(b, g, n_lat // tq),
        in_specs=[
            pl.BlockSpec((None, None, r, tq, d), lambda bi, gi, i: (bi, gi, 0, i, 0)),
            pl.BlockSpec((None, None, s, d), lambda bi, gi, i: (bi, gi, 0, 0)),
            pl.BlockSpec((None, None, s, d), lambda bi, gi, i: (bi, gi, 0, 0)),
            pl.BlockSpec((None, r * tq, 1), lambda bi, gi, i: (gi, 0, 0)),
            pl.BlockSpec(memory_space=pl.ANY),
        ],
        out_specs=pl.BlockSpec((None, None, r, tq, d), lambda bi, gi, i: (bi, gi, 0, i, 0)),
        out_shape=jax.ShapeDtypeStruct((b, g, r, s, d), MXU_DTYPE),
        input_output_aliases={4: 0},
        compiler_params=_cparams(("arbitrary", "arbitrary", "arbitrary")),
        name="window_attention",
    )(q5, k, v, sink_col, jnp.zeros((b, g, r, s, d), MXU_DTYPE))
    return o.reshape(b, hq, s, d)


def _toeplitz(a, n_q, n_k, off):
    length = a.shape[-1]
    lo = (n_q - 1) - off
    hi = (n_q + n_k - 1) - lo - length
    ap = jnp.pad(a, [(0, 0)] * (a.ndim - 1) + [(max(lo, 0), max(hi, 0))])
    ap = ap[..., max(-lo, 0):ap.shape[-1] - max(-hi, 0)]
    return jnp.stack([ap[..., n_q - 1 - q:n_q - 1 - q + n_k] for q in range(n_q)], axis=-2)


def _na_tables(rpb, rows):
    w = GRID_W
    kh = min(NA_ROWS, rows)
    configs = [(0, 0), (NA_QROWS, NA_QROWS - NA_ROWS // 2), (rows - NA_QROWS, rows - NA_KROWS)]
    qi = jnp.arange(NA_QROWS, dtype=jnp.int32)[:, None, None, None]
    qc = jnp.arange(w, dtype=jnp.int32)[None, :, None, None]
    kj = jnp.arange(NA_KROWS, dtype=jnp.int32)[None, None, :, None]
    kc = jnp.arange(w, dtype=jnp.int32)[None, None, None, :]
    full = (NA_QROWS, w, NA_KROWS, w)
    flat = (NA_QROWS * w, NA_KROWS * w)
    by_col = jnp.moveaxis(_toeplitz(rpb.astype(jnp.float32), w, w, NA_COLS - 1), 1, -1)
    tabs = []
    for r0, kr0 in configs:
        qr = r0 + qi
        kr = kr0 + kj
        rs = jnp.clip(qr - kh // 2, 0, rows - kh)
        cs = jnp.clip(qc - NA_COLS // 2, 0, w - NA_COLS)
        valid = (kr >= rs) & (kr < rs + kh) & (kc >= cs) & (kc < cs + NA_COLS)
        valid = jnp.broadcast_to(valid, full).reshape(flat)
        bias = _toeplitz(by_col, NA_QROWS, NA_KROWS, kr0 - r0 + NA_ROWS - 1)
        bias = jnp.transpose(bias, (0, 3, 1, 4, 2)).reshape((-1,) + flat)
        tabs.append(jnp.where(valid[None], bias, NEG_INF))
    return jnp.stack(tabs, axis=1)


def _na_kernel(q_ref, k_ref, v_ref, tab_ref, init_ref, o_ref, *, tq, tkw, n_lat, n_ctx, nq):
    del init_ref
    i = pl.program_id(2)
    kind = jnp.where(i == 0, 0, jnp.where(i == nq - 1, 2, 1))
    start = pl.multiple_of(jnp.clip(i * tq - (NA_ROWS // 2) * GRID_W, 0, n_lat - tkw), GRID_W)
    k_loc = k_ref[pl.ds(start, tkw), :]
    v_loc = v_ref[pl.ds(start, tkw), :]
    k_ctx = k_ref[n_lat:n_lat + n_ctx, :]
    v_ctx = v_ref[n_lat:n_lat + n_ctx, :]
    parts = [(lo, lo + tq // ATTN_SPLIT) for lo in range(0, tq, tq // ATTN_SPLIT)]
    scores = [(_dot_nt(q_ref[lo:hi, :], k_loc), _dot_nt(q_ref[lo:hi, :], k_ctx)) for lo, hi in parts]
    for (lo, hi), (s_loc, s_ctx) in zip(parts, scores):
        s_loc = s_loc + tab_ref[kind, lo:hi, :]
        m = jnp.maximum(jnp.max(s_loc, axis=-1, keepdims=True), jnp.max(s_ctx, axis=-1, keepdims=True))
        p_loc = jnp.exp(s_loc - m)
        p_ctx = jnp.exp(s_ctx - m)
        denom = jnp.sum(p_loc, axis=-1, keepdims=True) + jnp.sum(p_ctx, axis=-1, keepdims=True)
        o = (_dot(p_loc.astype(MXU_DTYPE), v_loc) + _dot(p_ctx.astype(MXU_DTYPE), v_ctx)) / denom
        o_ref[lo:hi, :] = o.astype(o_ref.dtype)


def _neighbourhood_attention(q, k, v, tables, n_lat, n_ctx):
    b, hq, s, d = q.shape
    tq = NA_QROWS * GRID_W
    tkw = NA_KROWS * GRID_W
    nq = n_lat // tq
    return pl.pallas_call(
        functools.partial(_na_kernel, tq=tq, tkw=tkw, n_lat=n_lat, n_ctx=n_ctx, nq=nq),
        grid=(b, hq, nq),
        in_specs=[
            pl.BlockSpec((None, None, tq, d), lambda bi, h, i: (bi, h, i, 0)),
            pl.BlockSpec((None, None, s, d), lambda bi, h, i: (bi, h, 0, 0)),
            pl.BlockSpec((None, None, s, d), lambda bi, h, i: (bi, h, 0, 0)),
            pl.BlockSpec((None, 3, tq, tkw), lambda bi, h, i: (h, 0, 0, 0)),
            pl.BlockSpec(memory_space=pl.ANY),
        ],
        out_specs=pl.BlockSpec((None, None, tq, d), lambda bi, h, i: (bi, h, i, 0)),
        out_shape=jax.ShapeDtypeStruct((b, hq, s, d), MXU_DTYPE),
        input_output_aliases={4: 0},
        compiler_params=_cparams(("arbitrary", "arbitrary", "arbitrary")),
        name="neighbourhood_attention",
    )(q, k, v, tables, jnp.zeros((b, hq, s, d), MXU_DTYPE))


def _outproj_kernel(o1_ref, o2_ref, w_ref, x_ref, mod_ref, rw_ref, rb_ref,
                    xo_ref, h_ref, lg_ref, *, tm, n_lat, n_heads, transposed):
    row0 = pl.program_id(1) * tm
    if transposed:
        yt = jnp.concatenate([o1_ref[...].reshape(-1, tm), o2_ref[...].reshape(-1, tm)], axis=0)
        z = lax.dot_general(yt, w_ref[...], (((0,), (0,)), ((), ())), preferred_element_type=jnp.float32)
    else:
        y = jnp.concatenate([o1_ref[h] for h in range(n_heads)] + [o2_ref[h] for h in range(n_heads)], axis=-1)
        z = _dot(y, w_ref[...])
    rows = row0 + lax.broadcasted_iota(jnp.int32, (tm, 1), 0)
    gate = jnp.where(rows >= n_lat, mod_ref[6:7, :], mod_ref[4:5, :])
    xn = x_ref[...] + gate * z
    xo_ref[...] = xn
    h = _modulated_norm(xn, mod_ref, row0, n_lat, 0, 1, 2, 3)
    _store_row_tiles(h_ref, h)
    lg_ref[...] = jnp.dot(h, rw_ref[...], preferred_element_type=jnp.float32,
                          precision=lax.Precision.HIGHEST) + rb_ref[...]


def _outproj(o1, o2, w_out, xs, modv, router_w, router_b, n_lat, transposed):
    b, s, dm = xs.shape
    tm = ROW_TILE
    nh = o1.shape[1]
    nl = router_w.shape[1]
    if transposed:
        o_spec = pl.BlockSpec((None, nh, o1.shape[2], tm), lambda bi, j: (bi, 0, 0, j))
    else:
        o_spec = pl.BlockSpec((None, nh, tm, o1.shape[3]), lambda bi, j: (bi, 0, j, 0))
    return pl.pallas_call(
        functools.partial(_outproj_kernel, tm=tm, n_lat=n_lat, n_heads=nh, transposed=transposed),
        grid=(b, s // tm),
        in_specs=[
            o_spec,
            o_spec,
            pl.BlockSpec(w_out.shape, lambda bi, j: (0, 0)),
            pl.BlockSpec((None, tm, dm), lambda bi, j: (bi, j, 0)),
            pl.BlockSpec((None, 8, dm), lambda bi, j: (bi, 0, 0)),
            pl.BlockSpec(router_w.shape, lambda bi, j: (0, 0)),
            pl.BlockSpec(router_b.shape, lambda bi, j: (0, 0)),
        ],
        out_specs=[pl.BlockSpec((None, tm, dm), lambda bi, j: (bi, j, 0)),
                   pl.BlockSpec((None, tm * ROW_TILES, LANES), lambda bi, j: (bi, j, 0)),
                   pl.BlockSpec((None, tm, nl), lambda bi, j: (bi, j, 0))],
        out_shape=[jax.ShapeDtypeStruct((b, s, dm), jnp.float32),
                   jax.ShapeDtypeStruct((b, s * ROW_TILES, LANES), jnp.float32),
                   jax.ShapeDtypeStruct((b, s, nl), jnp.float32)],
        input_output_aliases={3: 0},
        compiler_params=_cparams(("arbitrary", "arbitrary")),
        name="outproj_residual_norm_router",
    )(o1, o2, w_out, xs, modv, router_w, router_b)


def _expert_kernel(be_ref, tok0_ref, tokn_ref, h_hbm, win_ref, bin_ref, wout_ref, bout_ref,
                   y_ref, xbuf, sems, wi_sc, wo_sc, *, blk, nb):
    i = pl.program_id(0)
    slot = lax.rem(i, 2)
    rt = ROW_TILES

    @pl.when(jnp.logical_or(i == 0, be_ref[i] != be_ref[jnp.maximum(i - 1, 0)]))
    def _():
        wi_sc[...] = win_ref[...].astype(wi_sc.dtype)
        wo_sc[...] = wout_ref[...].astype(wo_sc.dtype)

    def gather(tok_ref, dst_slot):
        for rr in range(blk):
            src = pl.multiple_of(tok_ref[0, rr], rt)
            pltpu.make_async_copy(h_hbm.at[pl.ds(src, rt)], xbuf.at[dst_slot, pl.ds(rr * rt, rt)],
                                  sems.at[dst_slot]).start()

    @pl.when(i == 0)
    def _():
        gather(tok0_ref, 0)

    @pl.when(i + 1 < nb)
    def _():
        gather(tokn_ref, 1 - slot)

    pltpu.make_async_copy(h_hbm.at[pl.ds(0, blk * rt)], xbuf.at[slot], sems.at[slot]).wait()

    x = _load_row_tiles(xbuf.at[slot], blk).astype(MXU_DTYPE)
    u = _dot(x, wi_sc[...]) + bin_ref[...]
    glu = jnp.minimum(u[:, :D_EXPERT], SWIGLU_LIMIT)
    lin = jnp.clip(u[:, D_EXPERT:], -SWIGLU_LIMIT, SWIGLU_LIMIT)
    act = glu * jax.nn.sigmoid(SWIGLU_ALPHA * glu) * (lin + 1.0)
    _store_row_tiles(y_ref, _dot(act.astype(MXU_DTYPE), wo_sc[...]) + bout_ref[...])


def _experts(h_tiles, blk_expert, tok_buf, w_in, b_in, w_out, b_out, layer):
    dm = w_in.shape[2]
    blk = MOE_ROWS
    rt = ROW_TILES
    nb = blk_expert.shape[0]
    de2 = w_in.shape[-1]
    tok3 = tok_buf.reshape(nb, 1, blk)
    grid_spec = pltpu.PrefetchScalarGridSpec(
        num_scalar_prefetch=1,
        grid=(nb,),
        in_specs=[
            pl.BlockSpec((None, 1, blk), lambda i, be: (0, 0, 0), memory_space=pltpu.SMEM),
            pl.BlockSpec((None, 1, blk), lambda i, be: (jnp.minimum(i + 1, nb - 1), 0, 0),
                         memory_space=pltpu.SMEM),
            pl.BlockSpec(memory_space=pl.ANY),
            pl.BlockSpec((None, None, dm, de2), lambda i, be: (layer, be[i], 0, 0)),
            pl.BlockSpec((None, 1, de2), lambda i, be: (be[i], 0, 0)),
            pl.BlockSpec((None, None, de2 // 2, dm), lambda i, be: (layer, be[i], 0, 0)),
            pl.BlockSpec((None, 1, dm), lambda i, be: (be[i], 0, 0)),
        ],
        out_specs=pl.BlockSpec((blk * rt, LANES), lambda i, be: (i, 0)),
        scratch_shapes=[pltpu.VMEM((2, blk * rt, LANES), jnp.float32), pltpu.SemaphoreType.DMA((2,)),
                        pltpu.VMEM((dm, de2), MXU_DTYPE), pltpu.VMEM((de2 // 2, dm), MXU_DTYPE)],
    )
    return pl.pallas_call(
        functools.partial(_expert_kernel, blk=blk, nb=nb),
        grid_spec=grid_spec,
        out_shape=jax.ShapeDtypeStruct((nb * blk * rt, LANES), jnp.float32),
        compiler_params=_cparams(("arbitrary",)),
        name="routed_experts",
    )(blk_expert, tok3, tok3, h_tiles, w_in, b_in.reshape(N_EXPERTS, 1, de2), w_out,
      b_out.reshape(N_EXPERTS, 1, dm))


def _combine_kernel(pos0_ref, posn_ref, y_hbm, x_ref, mod_ref, w_ref, o_ref, gbuf, sems, *, tc, nsteps, n_lat,
                    steps_per_batch):
    i = pl.program_id(0)
    slot = lax.rem(i, 2)

    rt = ROW_TILES

    def gather(pos_ref, dst_slot):
        def issue(t, carry):
            for kk in range(TOP_K):
                src = pl.multiple_of(pos_ref[0, t * TOP_K + kk], rt)
                dst = pl.multiple_of(t * rt, rt)
                pltpu.make_async_copy(y_hbm.at[pl.ds(src, rt)], gbuf.at[dst_slot, kk, pl.ds(dst, rt)],
                                      sems.at[dst_slot]).start(priority=kk % 2)
            return carry
        lax.fori_loop(0, tc, issue, 0, unroll=8)

    @pl.when(i == 0)
    def _():
        gather(pos0_ref, 0)

    @pl.when(i + 1 < nsteps)
    def _():
        gather(posn_ref, 1 - slot)

    for kk in range(TOP_K):
        pltpu.make_async_copy(y_hbm.at[pl.ds(0, tc * rt)], gbuf.at[slot, kk], sems.at[slot]).wait()

    w = w_ref[...]
    y = _load_row_tiles(gbuf.at[slot, 0], tc) * w[:, 0:1]
    for kk in range(1, TOP_K):
        y = y + _load_row_tiles(gbuf.at[slot, kk], tc) * w[:, kk:kk + 1]
    row0 = lax.rem(i, steps_per_batch) * tc
    rows = row0 + lax.broadcasted_iota(jnp.int32, (tc, 1), 0)
    gate = jnp.where(rows >= n_lat, mod_ref[7:8, :], mod_ref[5:6, :])
    o_ref[...] = x_ref[...] + gate * y


def _combine(y_sorted, pos, weights, xs_flat, modv, s, n_lat):
    n, dm = xs_flat.shape
    tc = COMBINE_ROWS
    nsteps = n // tc
    spb = s // tc
    pos3 = pos.reshape(nsteps, 1, tc * TOP_K)
    return pl.pallas_call(
        functools.partial(_combine_kernel, tc=tc, nsteps=nsteps, n_lat=n_lat, steps_per_batch=spb),
        grid=(nsteps,),
        in_specs=[
            pl.BlockSpec((None, 1, tc * TOP_K), lambda i: (0, 0, 0), memory_space=pltpu.SMEM),
            pl.BlockSpec((None, 1, tc * TOP_K), lambda i: (jnp.minimum(i + 1, nsteps - 1), 0, 0),
                         memory_space=pltpu.SMEM),
            pl.BlockSpec(memory_space=pl.ANY),
            pl.BlockSpec((tc, dm), lambda i: (i, 0)),
            pl.BlockSpec((None, 8, dm), lambda i: (i // spb, 0, 0)),
            pl.BlockSpec((tc, TOP_K), lambda i: (i, 0)),
        ],
        out_specs=pl.BlockSpec((tc, dm), lambda i: (i, 0)),
        out_shape=jax.ShapeDtypeStruct((n, dm), jnp.float32),
        scratch_shapes=[pltpu.VMEM((2, TOP_K, tc * ROW_TILES, LANES), jnp.float32), pltpu.SemaphoreType.DMA((2,))],
        input_output_aliases={3: 0},
        compiler_params=_cparams(("arbitrary",)),
        name="expert_combine",
    )(pos3, pos3, y_sorted, xs_flat, modv, weights)


def _final_norm_kernel(x_ref, g_ref, o_ref):
    x = x_ref[...]
    o_ref[...] = x * lax.rsqrt(jnp.mean(x * x, axis=-1, keepdims=True) + NORM_EPS) * g_ref[...]


def _final_norm(xs, g, n_lat):
    b, s, dm = xs.shape
    tm = 512
    return pl.pallas_call(
        _final_norm_kernel,
        grid=(b, n_lat // tm),
        in_specs=[pl.BlockSpec((None, tm, dm), lambda bi, j: (bi, j, 0)),
                  pl.BlockSpec((1, dm), lambda bi, j: (0, 0))],
        out_specs=pl.BlockSpec((None, tm, dm), lambda bi, j: (bi, j, 0)),
        out_shape=jax.ShapeDtypeStruct((b, n_lat, dm), jnp.float32),
        compiler_params=_cparams(("arbitrary", "arbitrary")),
        name="final_norm",
    )(xs, g.reshape(1, dm))


def _swap_perm(width, dim):
    j = np.arange(width)
    half = dim // 2
    return np.where((j % dim) < half, j + half, j - half)


def _rope_tables(n_lat, n_ctx, dim):
    t = jnp.arange(n_lat, dtype=jnp.int32)
    row = (t // GRID_W).astype(jnp.float32)
    col = (t % GRID_W).astype(jnp.float32)
    quarter = dim // 4
    inv_freq = ROPE_BASE ** (-jnp.arange(quarter, dtype=jnp.float32) / quarter)
    ang = jnp.concatenate([row[:, None] * inv_freq, col[:, None] * inv_freq], axis=-1)
    cos, sin = jnp.cos(ang), jnp.sin(ang)
    reps = LANES // dim
    cos_t = jnp.tile(jnp.concatenate([cos, cos], axis=-1), (1, reps))
    sin_t = jnp.tile(jnp.concatenate([-sin, sin], axis=-1), (1, reps))
    cos_t = jnp.concatenate([cos_t, jnp.ones((n_ctx, LANES), jnp.float32)], axis=0)
    sin_t = jnp.concatenate([sin_t, jnp.zeros((n_ctx, LANES), jnp.float32)], axis=0)
    return cos_t, sin_t


def _mod_rows(mod_l, norm_g, b, first):
    lat, ctx = mod_l[:b], mod_l[b]
    a_lat = norm_g[None] * (1.0 + lat[:, first + 1])
    b_lat = lat[:, first]
    a_ctx = jnp.broadcast_to(norm_g * (1.0 + ctx[first + 1]), a_lat.shape)
    b_ctx = jnp.broadcast_to(ctx[first], a_lat.shape)
    g_mix_ctx = jnp.broadcast_to(ctx[2], a_lat.shape)
    g_ffn_ctx = jnp.broadcast_to(ctx[5], a_lat.shape)
    return jnp.stack([a_lat, b_lat, a_ctx, b_ctx, lat[:, 2], lat[:, 5], g_mix_ctx, g_ffn_ctx], axis=1)


def _route(logits, blk):
    n = logits.shape[0]
    nk = n * TOP_K
    top_val, top_idx = lax.top_k(logits, TOP_K)
    gate = jax.nn.softmax(top_val, axis=-1)
    expert = top_idx.reshape(-1)
    onehot = (expert[:, None] == jnp.arange(N_EXPERTS, dtype=jnp.int32)[None, :]).astype(jnp.int32)
    csum = jnp.cumsum(onehot, axis=0)
    rank = jnp.take_along_axis(csum, expert[:, None], axis=1)[:, 0] - 1
    sizes = csum[-1]
    padded = (sizes + blk - 1) // blk * blk
    pends = jnp.cumsum(padded)
    pstarts = pends - padded
    dest = pstarts[expert] + rank
    nb = -(-(nk + N_EXPERTS * (blk - 1)) // blk)
    blk_expert = jnp.minimum(jnp.searchsorted(pends, jnp.arange(nb, dtype=jnp.int32) * blk, side='right'),
                             N_EXPERTS - 1).astype(jnp.int32)
    order = jnp.argsort(expert)
    starts = jnp.cumsum(sizes) - sizes
    slot = jnp.arange(nb * blk, dtype=jnp.int32)
    slot_expert = jnp.repeat(blk_expert, blk)
    slot_rank = slot - pstarts[slot_expert]
    src = jnp.minimum(starts[slot_expert] + slot_rank, nk - 1)
    tok_buf = jnp.where(slot_rank < sizes[slot_expert], order[src] // TOP_K, 0).astype(jnp.int32)
    return blk_expert, tok_buf, gate, dest.astype(jnp.int32)


def kernel(x, c, ctx, c_ctx, mod_w, mod_b, norm_mix, norm_ffn, ab_w_in, ab_w_out, a_sink, b_rpb,
           cd_w_in, c_q_norm, c_w_q_b, c_kv_norm, c_w_kv_b, d_q_norm, d_k_norm, cd_w_out,
           router_w, router_b, exp_w_in, exp_b_in, exp_w_out, exp_b_out, final_norm):
    b, t, dm = x.shape
    assert dm == ROW_TILES * LANES
    n_ctx = ctx.shape[1]
    s = t + n_ctx
    depth = mod_w.shape[0]
    rows = t // GRID_W
    d = HEAD_DIM
    f32 = jnp.float32

    xs = jnp.concatenate([x, ctx], axis=1)

    cond = jnp.zeros((8, dm), f32).at[:b].set(c).at[b].set(c_ctx)
    mod_all = _modulation(cond, mod_w, mod_b)[:, :b + 1].reshape(depth, b + 1, 6, dm)

    cos_t, sin_t = _rope_tables(t, n_ctx, d)
    cosm_t, sinm_t = _rope_tables(t, n_ctx, C_ROPE)

    router_w_p = jnp.zeros((depth, dm, LANES), f32).at[:, :, :N_EXPERTS].set(router_w)
    router_b_p = jnp.zeros((depth, 1, LANES), f32).at[:, 0, :N_EXPERTS].set(router_b)

    p64_512 = _swap_perm(512, d)
    p64_128 = _swap_perm(128, d)
    p32 = _swap_perm(C_ROPE, C_ROPE)

    for layer in range(depth):
        i = layer // 2
        modv_mix = _mod_rows(mod_all[layer], norm_mix[layer], b, 0)
        modv_ffn = _mod_rows(mod_all[layer], norm_ffn[layer], b, 3)
        if layer % 2 == 0:
            w = ab_w_in[i]
            w_ext = jnp.concatenate([w, w[:, 0:512][:, p64_512], w[:, 512:640][:, p64_128]], axis=1).astype(MXU_DTYPE)
            qa, ka, va, qb, kb, vb = _proj_ab(xs, modv_mix, w_ext, cos_t, sin_t, t)
            sink_gr = a_sink[i].astype(f32).reshape(A_KV_HEADS, A_HEADS // A_KV_HEADS)
            r = A_HEADS // A_KV_HEADS
            sink_win = jnp.repeat(sink_gr, WIN_Q, axis=1).reshape(A_KV_HEADS, r * WIN_Q, 1)
            sink_ctx = jnp.repeat(sink_gr, n_ctx, axis=1).reshape(A_KV_HEADS, r * n_ctx, 1)
            o1 = _window_attention(qa, ka, va, sink_win, t, n_ctx)
            o1 = _flash(qa, ka, va, groups=A_KV_HEADS, tq=n_ctx, tk=n_ctx, q_row0=t, n_q=n_ctx,
                        kv_row0=t, n_kv=n_ctx, sink=sink_ctx, out=o1)
            tables = _na_tables(b_rpb[i], rows)
            o2 = _neighbourhood_attention(qb, kb, vb, tables, t, n_ctx)
            o2 = _flash(qb, kb, vb, groups=B_HEADS, tq=n_ctx, tk=n_ctx, q_row0=t, n_q=n_ctx,
                        kv_row0=t, n_kv=n_ctx, out=o2)
            w_out = ab_w_out[i].astype(MXU_DTYPE)
        else:
            w = cd_w_in[i]
            base = C_Q_RANK + C_KV_RANK
            o_qd, o_kd, o_vd = base + C_ROPE, base + C_ROPE + 512, base + C_ROPE + 640
            wqd, wkd = w[:, o_qd:o_qd + 512], w[:, o_kd:o_kd + 128]
            gq = jnp.tile(d_q_norm[i].astype(f32), D_HEADS)
            gk = jnp.tile(d_k_norm[i].astype(f32), D_KV_HEADS)
            krope = w[:, base:base + C_ROPE]
            pad = jnp.zeros((dm, LANES - C_ROPE), f32)
            w1 = jnp.concatenate([
                w[:, 0:base], wqd, wkd, w[:, o_vd:o_vd + 128],
                (wqd * gq[None])[:, p64_512], (wkd * gk[None])[:, p64_128],
                krope, pad, krope[:, p32], pad], axis=1).astype(MXU_DTYPE)
            dq = C_NOPE + C_ROPE
            wq = c_w_q_b[i].reshape(C_Q_RANK, C_HEADS, dq)
            wq_nope = wq[:, :, :C_NOPE].reshape(C_Q_RANK, C_HEADS * C_NOPE)
            wq_rope = wq[:, :, C_NOPE:].reshape(C_Q_RANK, C_HEADS * C_ROPE)
            wq2 = jnp.concatenate([wq_nope, wq_rope, wq_rope[:, _swap_perm(C_HEADS * C_ROPE, C_ROPE)]],
                                  axis=1).astype(MXU_DTYPE)
            wkv = c_w_kv_b[i].reshape(C_KV_RANK, C_HEADS, C_NOPE + C_V)
            wkv2 = jnp.concatenate([wkv[:, :, :C_NOPE].reshape(C_KV_RANK, -1),
                                    wkv[:, :, C_NOPE:].reshape(C_KV_RANK, -1)], axis=1).astype(MXU_DTYPE)
            qc, kc, vc, qd, kd, vd = _proj_cd(
                xs, modv_mix, w1, wq2, wkv2, c_q_norm[i].astype(f32).reshape(1, -1),
                c_kv_norm[i].astype(f32).reshape(1, -1), gq.reshape(1, -1), gk.reshape(1, -1),
                cos_t, sin_t, cosm_t, sinm_t, t)
            o1 = _flash_t(qc, kc, vc, groups=C_HEADS, tq=FLASH_M, tk=FLASH_TK, q_row0=0, n_q=t, kv_row0=0, n_kv=s,
                          out=jnp.zeros((b, C_HEADS, C_V, s), MXU_DTYPE))
            o1 = _flash_t(qc, kc, vc, groups=C_HEADS, tq=n_ctx, tk=n_ctx, q_row0=t, n_q=n_ctx,
                          kv_row0=t, n_kv=n_ctx, out=o1)
            rd = D_HEADS // D_KV_HEADS
            o2 = _flash_t(qd, kd, vd, groups=D_KV_HEADS, tq=FLASH_M // rd, tk=FLASH_TK, q_row0=0, n_q=t,
                          kv_row0=0, n_kv=s, out=jnp.zeros((b, D_HEADS, d, s), MXU_DTYPE))
            o2 = _flash_t(qd, kd, vd, groups=D_KV_HEADS, tq=n_ctx, tk=n_ctx, q_row0=t, n_q=n_ctx,
                          kv_row0=t, n_kv=n_ctx, out=o2)
            w_out = cd_w_out[i].astype(MXU_DTYPE)

        xs, h2, logits = _outproj(o1, o2, w_out, xs, modv_ffn, router_w_p[layer], router_b_p[layer], t,
                                  transposed=layer % 2 == 1)

        n = b * s
        blk_expert, tok_buf, weights, pos = _route(logits.reshape(n, LANES)[:, :N_EXPERTS], MOE_ROWS)
        y_sorted = _experts(h2.reshape(n * ROW_TILES, LANES), blk_expert, tok_buf * ROW_TILES,
                            exp_w_in.astype(f32), exp_b_in[layer].astype(f32),
                            exp_w_out.astype(f32), exp_b_out[layer].astype(f32), layer)
        xs = _combine(y_sorted, pos * ROW_TILES, weights, xs.reshape(n, dm), modv_ffn, s, t).reshape(b, s, dm)

    return _final_norm(xs, final_norm.astype(f32), t)
```
